```python
import math
import jax, jax.numpy as jnp
from jax import lax
import numpy as np


D_MODEL = 4096
BATCH = 8
SEQ = 2048
DEPTH = 1
DEC_BATCH = 2
DEC_SEQ = 8192
PAST_LEN = 128

MIX_WIDTH = D_MODEL
ATTN_WIDTH = MIX_WIDTH // 2
POOL_WIDTH = MIX_WIDTH - ATTN_WIDTH
N_DIFF_HEADS = 8
HEAD_DIM = ATTN_WIDTH // N_DIFF_HEADS // 2
V_HEAD_DIM = 2 * HEAD_DIM
ROT_DIM = HEAD_DIM // 4
ROPE_THETA = 500000.0
POOL_WINDOWS = (2, 4, 8, 16)
N_POOL_GROUPS = len(POOL_WINDOWS)
POOL_GROUP_WIDTH = POOL_WIDTH // N_POOL_GROUPS
IN_COLS = 3 * ATTN_WIDTH + POOL_WIDTH
N_GROUPS = 4
EXPERTS_PER_GROUP = 8
N_EXPERTS = N_GROUPS * EXPERTS_PER_GROUP
TOP_K = 2
D_FF_EXPERT = D_MODEL // 4
MOE_BLOCK = 128
Q_BLOCK = 128
EPS = 1e-6

kernel_name = 'hybrid_diffattn_pool_hmoe_encoder'


def rmsnorm(x, g):
    xf = x.astype(jnp.float32)
    out = xf * lax.rsqrt(jnp.mean(xf * xf, axis=-1, keepdims=True) + EPS)
    return (out * g.astype(jnp.float32)).astype(x.dtype)


def lambda_init_fn(layer_idx):
    return 0.8 - 0.6 * math.exp(-0.3 * layer_idx)


def partial_rope(x, seq_len):
    half = ROT_DIM // 2
    inv_freq = ROPE_THETA ** (-jnp.arange(half, dtype=jnp.float32) * 2.0 / ROT_DIM)
    ang = jnp.arange(seq_len, dtype=jnp.float32)[:, None] * inv_freq[None, :]
    cos, sin = jnp.cos(ang), jnp.sin(ang)
    xr = x[..., :ROT_DIM].astype(jnp.float32)
    x1, x2 = xr[..., :half], xr[..., half:]
    rot = jnp.concatenate([x1 * cos - x2 * sin, x2 * cos + x1 * sin], axis=-1).astype(x.dtype)
    return jnp.concatenate([rot, x[..., ROT_DIM:]], axis=-1)


def diff_attention(q, k, v, lam, subln_g, lambda_init):
    B, H, _, S, d = q.shape
    nb = S // Q_BLOCK
    scale = d ** -0.5
    qb = q.reshape(B, H, 2, nb, Q_BLOCK, d).transpose(3, 0, 1, 2, 4, 5)

    def block(qblk):
        s = jnp.einsum('bhcqd,bhckd->bhcqk', qblk, k, preferred_element_type=jnp.float32) * scale
        p = jax.nn.softmax(s, axis=-1)
        w = p[:, :, 0] - lam * p[:, :, 1]
        return jnp.einsum('bhqk,bhkd->bhqd', w.astype(v.dtype), v)

    o = lax.map(block, qb)
    o = o.transpose(1, 2, 0, 3, 4).reshape(B, H, S, 2 * d)
    o = rmsnorm(o, subln_g) * (1.0 - lambda_init)
    return o.transpose(0, 2, 1, 3).reshape(B, S, H * 2 * d)


def multiscale_pool(u, w_pool, pool_scale):
    B, S, C = u.shape
    uf = u.astype(jnp.float32)
    cs = jnp.concatenate([jnp.zeros((B, 1, C), jnp.float32), jnp.cumsum(uf, axis=1)], axis=1)
    t = jnp.arange(S)
    outs = []
    for gi, w in enumerate(POOL_WINDOWS):
        lo_c, hi_c = gi * POOL_GROUP_WIDTH, (gi + 1) * POOL_GROUP_WIDTH
        left = w // 2
        right = w - 1 - left
        lo = jnp.clip(t - left, 0, S - 1)
        hi = jnp.clip(t + right, 0, S - 1)
        csg = cs[:, :, lo_c:hi_c]
        cnt = (hi - lo + 1).astype(jnp.float32)[None, :, None]
        mean = (csg[:, hi + 1] - csg[:, lo]) / cnt
        delta = (mean - uf[:, :, lo_c:hi_c]).astype(u.dtype)
        outs.append(jnp.einsum('bsc,ce->bse', delta, w_pool[gi]))
    return jnp.concatenate(outs, axis=-1) * pool_scale


def hierarchical_moe(x, w_rg, b_rg, w_re, b_re, w_gate, w_up, w_down):
    B, S, D = x.shape
    T = B * S
    xf = x.reshape(T, D)
    g_logits = jnp.einsum('td,dg->tg', xf, w_rg, preferred_element_type=jnp.float32) + b_rg.astype(jnp.float32)
    g_prob = jax.nn.softmax(g_logits, axis=-1)
    g_idx = jnp.argmax(g_logits, axis=-1)
    g_w = jnp.take_along_axis(g_prob, g_idx[:, None], axis=-1)
    e_logits = jnp.einsum('td,de->te', xf, w_re, preferred_element_type=jnp.float32) + b_re.astype(jnp.float32)
    e_logits = e_logits.reshape(T, N_GROUPS, EXPERTS_PER_GROUP)
    e_logits = jnp.take_along_axis(e_logits, g_idx[:, None, None], axis=1)[:, 0]
    e_prob = jax.nn.softmax(e_logits, axis=-1)
    top_p, top_e = lax.top_k(e_prob, TOP_K)
    top_p = top_p / jnp.sum(top_p, axis=-1, keepdims=True)
    gate = g_w * top_p
    expert_id = (g_idx[:, None] * EXPERTS_PER_GROUP + top_e).astype(jnp.int32)

    A = T * TOP_K
    eid = expert_id.reshape(A)
    tok = jnp.repeat(jnp.arange(T, dtype=jnp.int32), TOP_K)
    gw = gate.reshape(A)
    order = jnp.argsort(eid)
    eid_s = eid[order]
    counts = jnp.bincount(eid, length=N_EXPERTS)
    starts = jnp.cumsum(counts) - counts
    padded = (counts + MOE_BLOCK - 1) // MOE_BLOCK * MOE_BLOCK
    pends = jnp.cumsum(padded)
    pstarts = pends - padded
    dest = pstarts[eid_s] + jnp.arange(A) - starts[eid_s]
    n_blocks = -(-A // MOE_BLOCK) + N_EXPERTS
    P = n_blocks * MOE_BLOCK
    slot_tok = jnp.full((P,), T, jnp.int32).at[dest].set(tok[order])
    slot_gate = jnp.zeros((P,), jnp.float32).at[dest].set(gw[order])
    block_start = jnp.arange(n_blocks) * MOE_BLOCK
    block_exp = jnp.clip(jnp.searchsorted(pends, block_start, side='right'), 0, N_EXPERTS - 1)
    x_pad = jnp.concatenate([xf, jnp.zeros((1, D), xf.dtype)], axis=0)
    xs = x_pad[slot_tok].reshape(n_blocks, MOE_BLOCK, D)

    def run_block(args):
        xb, e = args
        h = jax.nn.silu(xb @ w_gate[e]) * (xb @ w_up[e])
        return h @ w_down[e]

    ys = lax.map(run_block, (xs, block_exp)).reshape(P, D)
    ys = ys * slot_gate[:, None].astype(ys.dtype)
    out = jax.ops.segment_sum(ys, slot_tok, num_segments=T + 1)[:T]
    return out.reshape(B, S, D)


def trunk(x, norm_mix, w_in, q_norm, k_norm, lambda_q1, lambda_k1, lambda_q2, lambda_k2,
          subln, w_pool, pool_scale, w_out, norm_ffn, w_router_group, b_router_group,
          w_router_expert, b_router_expert, w_gate, w_up, w_down):
    B, S, _ = x.shape
    for l in range(DEPTH):
        lam_init = lambda_init_fn(l)
        h = rmsnorm(x, norm_mix[l])
        proj = jnp.einsum('bsd,de->bse', h, w_in[l])
        q = proj[..., :ATTN_WIDTH]
        k = proj[..., ATTN_WIDTH:2 * ATTN_WIDTH]
        v = proj[..., 2 * ATTN_WIDTH:3 * ATTN_WIDTH]
        u = proj[..., 3 * ATTN_WIDTH:]
        q = q.reshape(B, S, N_DIFF_HEADS, 2, HEAD_DIM).transpose(0, 2, 3, 1, 4)
        k = k.reshape(B, S, N_DIFF_HEADS, 2, HEAD_DIM).transpose(0, 2, 3, 1, 4)
        v = v.reshape(B, S, N_DIFF_HEADS, V_HEAD_DIM).transpose(0, 2, 1, 3)
        q = partial_rope(rmsnorm(q, q_norm[l]), S)
        k = partial_rope(rmsnorm(k, k_norm[l]), S)
        lam = (jnp.exp(jnp.sum(lambda_q1[l].astype(jnp.float32) * lambda_k1[l].astype(jnp.float32)))
               - jnp.exp(jnp.sum(lambda_q2[l].astype(jnp.float32) * lambda_k2[l].astype(jnp.float32)))
               + lam_init)
        a = diff_attention(q, k, v, lam, subln[l], lam_init)
        p = multiscale_pool(u, w_pool[l], pool_scale[l])
        x = x + jnp.einsum('bse,ed->bsd', jnp.concatenate([a, p], axis=-1), w_out[l])
        x = x + hierarchical_moe(rmsnorm(x, norm_ffn[l]), w_router_group[l], b_router_group[l],
                                 w_router_expert[l], b_router_expert[l],
                                 w_gate[l], w_up[l], w_down[l])
    return x


def setup_inputs(seed: int = 0) -> dict:
    key = jax.random.key(seed)
    ks = jax.random.split(key, 24)
    f32 = jnp.float32
    L = DEPTH

    def nrm(k, shape, scale):
        return jax.random.normal(k, shape, f32) * scale

    return {
        'x_prompt': nrm(ks[0], (BATCH, SEQ, D_MODEL), 1.0),
        'x_sample': nrm(ks[1], (DEC_BATCH, DEC_SEQ, D_MODEL), 1.0),
        'norm_mix': 1.0 + nrm(ks[2], (L, D_MODEL), 0.01),
        'w_in': nrm(ks[3], (L, D_MODEL, IN_COLS), D_MODEL ** -0.5),
        'q_norm': 1.0 + nrm(ks[4], (L, HEAD_DIM), 0.01),
        'k_norm': 1.0 + nrm(ks[5], (L, HEAD_DIM), 0.01),
        'lambda_q1': nrm(ks[6], (L, HEAD_DIM), 0.1),
        'lambda_k1': nrm(ks[7], (L, HEAD_DIM), 0.1),
        'lambda_q2': nrm(ks[8], (L, HEAD_DIM), 0.1),
        'lambda_k2': nrm(ks[9], (L, HEAD_DIM), 0.1),
        'subln': 1.0 + nrm(ks[10], (L, V_HEAD_DIM), 0.01),
        'w_pool': nrm(ks[11], (L, N_POOL_GROUPS, POOL_GROUP_WIDTH, POOL_GROUP_WIDTH), POOL_GROUP_WIDTH ** -0.5),
        'pool_scale': 1.0 + nrm(ks[12], (L, POOL_WIDTH), 0.1),
        'w_out': nrm(ks[13], (L, MIX_WIDTH, D_MODEL), MIX_WIDTH ** -0.5),
        'norm_ffn': 1.0 + nrm(ks[14], (L, D_MODEL), 0.01),
        'w_router_group': nrm(ks[15], (L, D_MODEL, N_GROUPS), D_MODEL ** -0.5),
        'b_router_group': nrm(ks[16], (L, N_GROUPS), 0.01),
        'w_router_expert': nrm(ks[17], (L, D_MODEL, N_EXPERTS), D_MODEL ** -0.5),
        'b_router_expert': nrm(ks[18], (L, N_EXPERTS), 0.01),
        'w_gate': nrm(ks[19], (L, N_EXPERTS, D_MODEL, D_FF_EXPERT), D_MODEL ** -0.5),
        'w_up': nrm(ks[20], (L, N_EXPERTS, D_MODEL, D_FF_EXPERT), D_MODEL ** -0.5),
        'w_down': nrm(ks[21], (L, N_EXPERTS, D_FF_EXPERT, D_MODEL), D_FF_EXPERT ** -0.5),
    }


def reference(x_prompt, x_sample, norm_mix, w_in, q_norm, k_norm, lambda_q1, lambda_k1,
              lambda_q2, lambda_k2, subln, w_pool, pool_scale, w_out, norm_ffn,
              w_router_group, b_router_group, w_router_expert, b_router_expert,
              w_gate, w_up, w_down):
    y_prompt = trunk(x_prompt, norm_mix, w_in, q_norm, k_norm, lambda_q1, lambda_k1, lambda_q2,
                     lambda_k2, subln, w_pool, pool_scale, w_out, norm_ffn, w_router_group,
                     b_router_group, w_router_expert, b_router_expert, w_gate, w_up, w_down)
    y_sample = trunk(x_sample, norm_mix, w_in, q_norm, k_norm, lambda_q1, lambda_k1, lambda_q2,
                     lambda_k2, subln, w_pool, pool_scale, w_out, norm_ffn, w_router_group,
                     b_router_group, w_router_expert, b_router_expert, w_gate, w_up, w_down)
    return (y_prompt, y_sample)
```

```python
import functools
import math

import jax
import jax.numpy as jnp
import numpy as np
from jax import lax
from jax.experimental import pallas as pl
from jax.experimental.pallas import tpu as pltpu

EPS = 1e-6
ROPE_THETA = 500000.0
POOL_WINDOWS = (2, 4, 8, 16)
TOP_K = 2
POOL_HALO = 16
LANES = 128
VMEM_CAP_BYTES = 60000 * 1024
F32 = jnp.float32
BF16 = jnp.bfloat16


def _lambda_init(layer_idx):
    return 0.8 - 0.6 * math.exp(-0.3 * layer_idx)


def _vmem_limit(est_bytes):
    return int(min(VMEM_CAP_BYTES, est_bytes * 5 // 4 + (4 << 20)))


def _params(sem, est_bytes):
    return pltpu.CompilerParams(dimension_semantics=sem, vmem_limit_bytes=_vmem_limit(est_bytes))


def _inproj_kernel(x_ref, g_ref, w_ref, hg_ref, cos_ref, sa_ref, sb_ref, o_ref, xn_ref, *,
                   n_qk_tiles, hd, rot_half):
    j = pl.program_id(1)

    @pl.when(j == 0)
    def _():
        x = x_ref[...]
        ms = jnp.mean(x * x, axis=-1, keepdims=True)
        xn_ref[...] = (x * lax.rsqrt(ms + EPS) * g_ref[...]).astype(xn_ref.dtype)

    acc = jnp.dot(xn_ref[...], w_ref[...], preferred_element_type=F32)
    tn = acc.shape[1]

    @pl.when(j < n_qk_tiles)
    def _():
        cos = cos_ref[...]
        sa = sa_ref[...]
        sb = sb_ref[...]
        for c in range(tn // hd):
            blk = acc[:, c * hd:(c + 1) * hd]
            ms = jnp.mean(blk * blk, axis=-1, keepdims=True)
            y = blk * lax.rsqrt(ms + EPS) * hg_ref[:, c * hd:(c + 1) * hd]
            y = (y * cos + pltpu.roll(y, rot_half, 1) * sa
                 + pltpu.roll(y, hd - rot_half, 1) * sb)
            o_ref[:, c * hd:(c + 1) * hd] = y.astype(o_ref.dtype)

    @pl.when(j >= n_qk_tiles)
    def _():
        o_ref[...] = acc.astype(o_ref.dtype)


def _inproj(x2, gain, w_bf, head_gain, cos_t, sa_t, sb_t, *, seq, aw, hd, rot_half):
    T, D = x2.shape
    N = w_bf.shape[1]
    tm = min(512, seq)
    tn = min(1024, 2 * aw)
    n_qk_tiles = (2 * aw) // tn
    s_tiles = seq // tm
    est = 2 * tm * D * 4 + 2 * D * tn * 2 + tm * D * 2 + 2 * tm * tn * 2 + 2 * tm * tn * 4
    kern = functools.partial(_inproj_kernel, n_qk_tiles=n_qk_tiles, hd=hd, rot_half=rot_half)
    return pl.pallas_call(
        kern,
        grid=(T // tm, N // tn),
        in_specs=[
            pl.BlockSpec((tm, D), lambda i, j: (i, 0)),
            pl.BlockSpec((1, D), lambda i, j: (0, 0)),
            pl.BlockSpec((D, tn), lambda i, j: (0, j)),
            pl.BlockSpec((1, tn), lambda i, j: (0, jnp.minimum(j, n_qk_tiles - 1))),
            pl.BlockSpec((tm, hd), lambda i, j: (i % s_tiles, 0)),
            pl.BlockSpec((tm, hd), lambda i, j: (i % s_tiles, 0)),
            pl.BlockSpec((tm, hd), lambda i, j: (i % s_tiles, 0)),
        ],
        out_specs=pl.BlockSpec((tm, tn), lambda i, j: (i, j)),
        out_shape=jax.ShapeDtypeStruct((T, N), BF16),
        scratch_shapes=[pltpu.VMEM((tm, D), BF16)],
        compiler_params=_params(("arbitrary", "arbitrary"), est),
        name="inproj",
    )(x2, gain, w_bf, head_gain, cos_t, sa_t, sb_t)


def _attn_kernel(lam_ref, subln_ref, q_ref, k_ref, v_ref, o_ref,
                 e_ref, mc_ref, m_ref, l_ref, acc_ref, *, tk, hd, lam_init):
    seq = k_ref.shape[0]
    tq = q_ref.shape[0]
    nck = seq // tk
    m_ref[...] = jnp.full(m_ref.shape, -jnp.inf, F32)
    l_ref[...] = jnp.zeros(l_ref.shape, F32)
    q = q_ref[...]

    def pass1(c, carry):
        kc = k_ref[pl.ds(pl.multiple_of(c * tk, tk), tk), :]
        for comp in range(2):
            s = lax.dot_general(q[:, comp * hd:(comp + 1) * hd], kc[:, comp * hd:(comp + 1) * hd],
                                (((1,), (1,)), ((), ())), preferred_element_type=F32)
            m_prev = m_ref[comp, :, 0:1]
            l_prev = l_ref[comp, :, 0:1]
            m_new = jnp.maximum(m_prev, jnp.max(s, axis=1, keepdims=True))
            e = jnp.exp(s - m_new)
            l_new = l_prev * jnp.exp(m_prev - m_new) + jnp.sum(e, axis=1, keepdims=True)
            m_ref[comp] = jnp.broadcast_to(m_new, (tq, LANES))
            l_ref[comp] = jnp.broadcast_to(l_new, (tq, LANES))
            mc_ref[comp, c] = jnp.broadcast_to(m_new, (tq, LANES))
            e_ref[comp, c] = e
        return carry

    lax.fori_loop(0, nck, pass1, 0)

    lv = lam_ref[...]
    lam = (jnp.exp(jnp.sum(lv[0:1] * lv[1:2], axis=1, keepdims=True))
           - jnp.exp(jnp.sum(lv[2:3] * lv[3:4], axis=1, keepdims=True)) + lam_init)
    m0 = m_ref[0, :, 0:1]
    m1 = m_ref[1, :, 0:1]
    r0 = 1.0 / l_ref[0, :, 0:1]
    r1 = lam / l_ref[1, :, 0:1]
    acc_ref[...] = jnp.zeros(acc_ref.shape, F32)

    def pass2(c, carry):
        f0 = jnp.exp(mc_ref[0, c, :, 0:1] - m0) * r0
        f1 = jnp.exp(mc_ref[1, c, :, 0:1] - m1) * r1
        w = e_ref[0, c] * f0 - e_ref[1, c] * f1
        vc = v_ref[pl.ds(pl.multiple_of(c * tk, tk), tk), :]
        acc_ref[...] += jnp.dot(w.astype(vc.dtype), vc, preferred_element_type=F32)
        return carry

    lax.fori_loop(0, nck, pass2, 0)

    o = acc_ref[...]
    ms = jnp.mean(o * o, axis=-1, keepdims=True)
    o = o * lax.rsqrt(ms + EPS) * subln_ref[...]
    o_ref[...] = (o * (1.0 - lam_init)).astype(o_ref.dtype)


def _attention(proj, lamv, subln, *, batch, seq, aw, hd, lam_init):
    T = proj.shape[0]
    vd = 2 * hd
    n_heads = aw // vd
    tq = min(seq, 512 if seq <= 2048 else 256)
    tk = min(seq, 512)
    nck = seq // tk
    q_tiles = seq // tq
    k_col0 = aw // vd
    v_col0 = 2 * aw // vd
    est = (2 * 2 * seq * vd * 2 + 2 * seq * tq * 4 + 2 * nck * tq * LANES * 4
           + 4 * tq * LANES * 4 + tq * vd * 4 + 4 * tq * vd * 2 + 6 * tq * tk * 4)
    kern = functools.partial(_attn_kernel, tk=tk, hd=hd, lam_init=lam_init)
    return pl.pallas_call(
        kern,
        grid=(batch, n_heads, q_tiles),
        in_specs=[
            pl.BlockSpec((4, hd), lambda b, h, i: (0, 0)),
            pl.BlockSpec((1, vd), lambda b, h, i: (0, 0)),
            pl.BlockSpec((tq, vd), lambda b, h, i: (b * q_tiles + i, h)),
            pl.BlockSpec((seq, vd), lambda b, h, i: (b, k_col0 + h)),
            pl.BlockSpec((seq, vd), lambda b, h, i: (b, v_col0 + h)),
        ],
        out_specs=pl.BlockSpec((tq, vd), lambda b, h, i: (b * q_tiles + i, h)),
        out_shape=jax.ShapeDtypeStruct((T, aw), BF16),
        scratch_shapes=[
            pltpu.VMEM((2, nck, tq, tk), F32),
            pltpu.VMEM((2, nck, tq, LANES), F32),
            pltpu.VMEM((2, tq, LANES), F32),
            pltpu.VMEM((2, tq, LANES), F32),
            pltpu.VMEM((tq, vd), F32),
        ],
        compiler_params=_params(("arbitrary", "arbitrary", "arbitrary"), est),
        name="diff_attention",
    )(lamv, subln, proj, proj, proj)


def _pool_bands(rows):
    t = np.arange(rows)[:, None]
    jm = np.arange(rows)[None, :]
    jh = np.arange(POOL_HALO)[None, :]
    bm, bp, bn = [], [], []
    for w in POOL_WINDOWS:
        left = w // 2
        right = w - 1 - left
        bm.append((jm >= t - left) & (jm <= t + right))
        bp.append(jh - POOL_HALO >= t - left)
        bn.append(rows + jh <= t + right)
    as_bf = lambda m: jnp.asarray(np.stack(m).astype(np.float32), dtype=BF16)
    return as_bf(bm), as_bf(bp), as_bf(bn)


def _pool_kernel(u_ref, wp_ref, ps_ref, bm_ref, bp_ref, bn_ref, o_ref, *, rows):
    g = pl.program_id(1)
    seq = u_ref.shape[0]
    n_chunks = seq // rows
    for gi, w in enumerate(POOL_WINDOWS):
        left = w // 2
        right = w - 1 - left

        @pl.when(g == gi)
        def _(gi=gi, left=left, right=right):
            def body(r, carry):
                r0 = pl.multiple_of(r * rows, rows)
                main = u_ref[pl.ds(r0, rows), :]
                p0 = pl.multiple_of(jnp.maximum(r0 - POOL_HALO, 0), POOL_HALO)
                n0 = pl.multiple_of(jnp.minimum(r0 + rows, seq - POOL_HALO), POOL_HALO)
                prev = u_ref[pl.ds(p0, POOL_HALO), :]
                nxt = u_ref[pl.ds(n0, POOL_HALO), :]
                prev = jnp.where(r > 0, prev, jnp.zeros_like(prev))
                nxt = jnp.where(r < n_chunks - 1, nxt, jnp.zeros_like(nxt))
                ssum = (jnp.dot(bm_ref[gi], main, preferred_element_type=F32)
                        + jnp.dot(bp_ref[gi], prev, preferred_element_type=F32)
                        + jnp.dot(bn_ref[gi], nxt, preferred_element_type=F32))
                t = r0 + lax.broadcasted_iota(jnp.int32, (rows, 1), 0)
                lo = jnp.maximum(t - left, 0)
                hi = jnp.minimum(t + right, seq - 1)
                cnt = (hi - lo + 1).astype(F32)
                delta = ssum / cnt - main.astype(F32)
                y = jnp.dot(delta.astype(wp_ref.dtype), wp_ref[...], preferred_element_type=F32)
                o_ref[pl.ds(r0, rows), :] = (y * ps_ref[...]).astype(o_ref.dtype)
                return carry

            lax.fori_loop(0, n_chunks, body, 0)


def _pool(proj, w_pool_bf, pool_scale, bands, *, batch, seq, aw, pw):
    T = proj.shape[0]
    n_groups, gw, _ = w_pool_bf.shape
    rows = min(seq, 512)
    u_col0 = 3 * aw // gw
    bm, bp, bn = bands
    est = (2 * 2 * seq * gw * 2 + 2 * gw * gw * 2 + 2 * 4 * rows * (rows + 2 * POOL_HALO) * 2
           + 8 * rows * gw * 4)
    kern = functools.partial(_pool_kernel, rows=rows)
    return pl.pallas_call(
        kern,
        grid=(batch, n_groups),
        in_specs=[
            pl.BlockSpec((seq, gw), lambda b, g: (b, u_col0 + g)),
            pl.BlockSpec((None, gw, gw), lambda b, g: (g, 0, 0)),
            pl.BlockSpec((1, gw), lambda b, g: (0, g)),
            pl.BlockSpec(bm.shape, lambda b, g: (0, 0, 0)),
            pl.BlockSpec(bp.shape, lambda b, g: (0, 0, 0)),
            pl.BlockSpec(bn.shape, lambda b, g: (0, 0, 0)),
        ],
        out_specs=pl.BlockSpec((seq, gw), lambda b, g: (b, g)),
        out_shape=jax.ShapeDtypeStruct((T, pw), BF16),
        compiler_params=_params(("arbitrary", "arbitrary"), est),
        name="multiscale_pool",
    )(proj, w_pool_bf, pool_scale, bm, bp, bn)


def _outproj_kernel(a_ref, p_ref, wa_ref, wp_ref, x_ref, o_ref):
    acc = (jnp.dot(a_ref[...], wa_ref[...], preferred_element_type=F32)
           + jnp.dot(p_ref[...], wp_ref[...], preferred_element_type=F32))
    o_ref[...] = x_ref[...] + acc


def _outproj(a, p, wa_bf, wp_bf, x2):
    T, D = x2.shape
    aw = a.shape[1]
    pw = p.shape[1]
    tm = min(512, T)
    tn = min(1024, D)
    est = 2 * (tm * aw * 2 + tm * pw * 2 + (aw + pw) * tn * 2 + 2 * tm * tn * 4) + 2 * tm * tn * 4
    return pl.pallas_call(
        _outproj_kernel,
        grid=(T // tm, D // tn),
        in_specs=[
            pl.BlockSpec((tm, aw), lambda i, j: (i, 0)),
            pl.BlockSpec((tm, pw), lambda i, j: (i, 0)),
            pl.BlockSpec((aw, tn), lambda i, j: (0, j)),
            pl.BlockSpec((pw, tn), lambda i, j: (0, j)),
            pl.BlockSpec((tm, tn), lambda i, j: (i, j)),
        ],
        out_specs=pl.BlockSpec((tm, tn), lambda i, j: (i, j)),
        out_shape=jax.ShapeDtypeStruct((T, D), F32),
        compiler_params=_params(("arbitrary", "arbitrary"), est),
        name="outproj",
    )(a, p, wa_bf, wp_bf, x2)


def _router_kernel(xa_ref, xb_ref, g_ref, whi_ref, wlo_ref, b_ref, xn_ref, eid_ref, gate_ref, *,
                   a_tiles, n_groups, n_experts):
    i = pl.program_id(0)
    epg_shift = int(math.log2(n_experts // n_groups))

    def run(x_ref):
        x = x_ref[...]
        ms = jnp.mean(x * x, axis=-1, keepdims=True)
        xn = x * lax.rsqrt(ms + EPS) * g_ref[...]
        hi = xn.astype(BF16)
        xn_ref[...] = hi.astype(F32)
        lo = (xn - hi.astype(F32)).astype(BF16)
        whi = whi_ref[...]
        logits = (jnp.dot(hi, whi, preferred_element_type=F32)
                  + jnp.dot(lo, whi, preferred_element_type=F32)
                  + jnp.dot(hi, wlo_ref[...], preferred_element_type=F32)
                  + b_ref[...])
        lane = lax.broadcasted_iota(jnp.int32, logits.shape, 1)
        neg = jnp.float32(-jnp.inf)
        is_g = lane < n_groups
        gl = jnp.where(is_g, logits, neg)
        gmax = jnp.max(gl, axis=1, keepdims=True)
        g_idx = jnp.min(jnp.where(gl == gmax, lane, LANES), axis=1, keepdims=True)
        gsum = jnp.sum(jnp.where(is_g, jnp.exp(gl - gmax), 0.0), axis=1, keepdims=True)
        g_w = 1.0 / gsum
        e_lane = lane - n_groups
        grp = jnp.where(jnp.logical_and(e_lane >= 0, e_lane < n_experts),
                        lax.shift_right_arithmetic(e_lane, epg_shift), -1)
        sel = grp == g_idx
        el = jnp.where(sel, logits, neg)
        emax = jnp.max(el, axis=1, keepdims=True)
        ex = jnp.where(sel, jnp.exp(el - emax), 0.0)
        prob = ex / jnp.sum(ex, axis=1, keepdims=True)
        pm = jnp.where(sel, prob, -1.0)
        p1 = jnp.max(pm, axis=1, keepdims=True)
        i1 = jnp.min(jnp.where(pm == p1, lane, LANES), axis=1, keepdims=True)
        pm2 = jnp.where(lane == i1, -1.0, pm)
        p2 = jnp.max(pm2, axis=1, keepdims=True)
        i2 = jnp.min(jnp.where(pm2 == p2, lane, LANES), axis=1, keepdims=True)
        tsum = p1 + p2
        gate1 = g_w * (p1 / tsum)
        gate2 = g_w * (p2 / tsum)
        eid_ref[...] = jnp.where(lane == 0, i1 - n_groups, jnp.where(lane == 1, i2 - n_groups, 0))
        gate_ref[...] = jnp.where(lane == 0, gate1, jnp.where(lane == 1, gate2, 0.0))

    @pl.when(i < a_tiles)
    def _():
        run(xa_ref)

    @pl.when(i >= a_tiles)
    def _():
        run(xb_ref)


def _router(x1a, x1b, gain, whi, wlo, bias, *, n_groups, n_experts):
    Ta, D = x1a.shape
    Tb = x1b.shape[0]
    tt = min(256, Ta, Tb)
    a_tiles = Ta // tt
    b_tiles = Tb // tt
    T = Ta + Tb
    est = 2 * 2 * tt * D * 4 + 2 * tt * D * 4 + 4 * D * LANES * 2 + 6 * tt * D * 4
    kern = functools.partial(_router_kernel, a_tiles=a_tiles, n_groups=n_groups, n_experts=n_experts)
    return pl.pallas_call(
        kern,
        grid=(a_tiles + b_tiles,),
        in_specs=[
            pl.BlockSpec((tt, D), lambda i: (jnp.minimum(i, a_tiles - 1), 0)),
            pl.BlockSpec((tt, D), lambda i: (jnp.maximum(i - a_tiles, 0), 0)),
            pl.BlockSpec((1, D), lambda i: (0, 0)),
            pl.BlockSpec((D, LANES), lambda i: (0, 0)),
            pl.BlockSpec((D, LANES), lambda i: (0, 0)),
            pl.BlockSpec((1, LANES), lambda i: (0, 0)),
        ],
        out_specs=[
            pl.BlockSpec((tt, D), lambda i: (i, 0)),
            pl.BlockSpec((tt, LANES), lambda i: (i, 0)),
            pl.BlockSpec((tt, LANES), lambda i: (i, 0)),
        ],
        out_shape=[
            jax.ShapeDtypeStruct((T, D), F32),
            jax.ShapeDtypeStruct((T, LANES), jnp.int32),
            jax.ShapeDtypeStruct((T, LANES), F32),
        ],
        compiler_params=_params(("arbitrary",), est),
        name="router",
    )(x1a, x1b, gain, whi, wlo, bias)


def _moe_plan(eid2, *, tm, n_experts):
    T = eid2.shape[0]
    A = T * TOP_K
    eflat = eid2.reshape(A)
    order = jnp.argsort(eflat, stable=True).astype(jnp.int32)
    sorted_e = eflat[order]
    ends = jnp.searchsorted(sorted_e, jnp.arange(n_experts, dtype=jnp.int32), side="right").astype(jnp.int32)
    starts = jnp.concatenate([jnp.zeros((1,), jnp.int32), ends[:-1]])
    counts = ends - starts
    ntile_e = (counts + tm - 1) // tm
    tile_end = jnp.cumsum(ntile_e).astype(jnp.int32)
    tile_base = tile_end - ntile_e
    n_tiles = tile_end[-1]
    G = -(-A // tm) + n_experts
    ti = jnp.arange(G, dtype=jnp.int32)
    valid = ti < n_tiles
    te = jnp.clip(jnp.searchsorted(tile_end, ti, side="right"), 0, n_experts - 1).astype(jnp.int32)
    te = jnp.where(valid, te, te[n_tiles - 1])
    j = ti - tile_base[te]
    tstart = starts[te] + j * tm
    tlen = jnp.where(valid, jnp.clip(counts[te] - j * tm, 0, tm), 0).astype(jnp.int32)
    r = jnp.arange(tm, dtype=jnp.int32)
    sidx = jnp.clip(tstart[:, None] + r[None, :], 0, A - 1)
    tok_tbl = jnp.where(r[None, :] < tlen[:, None], order[sidx] // TOP_K, 0).astype(jnp.int32)
    rank = jnp.arange(A, dtype=jnp.int32) - starts[sorted_e]
    row = tile_base[sorted_e] * tm + rank
    pos = jnp.zeros((A,), jnp.int32).at[order].set(row).reshape(T, TOP_K)
    return te, tlen, n_tiles.reshape(1), tok_tbl, pos, G


def _dispatch_kernel(tbl_ref, xn_ref, xs_ref, idx_ref, sem_idx, sem_rows, *, tm, n_tiles):
    i = pl.program_id(0)
    slot = i % 2

    def idx_copy(tile, s):
        return pltpu.make_async_copy(tbl_ref.at[tile], idx_ref.at[s], sem_idx.at[s])

    def rows_wait(tile, s):
        dst = xs_ref.at[pl.ds(tile * tm, tm)]
        pltpu.make_async_copy(dst, dst, sem_rows.at[s]).wait()

    @pl.when(i == 0)
    def _():
        idx_copy(0, 0).start()

    idx_copy(i, slot).wait()

    @pl.when(i + 1 < n_tiles)
    def _():
        idx_copy(i + 1, 1 - slot).start()

    def body(r, carry):
        tok = idx_ref[slot, r]
        pltpu.make_async_copy(xn_ref.at[tok], xs_ref.at[i * tm + r], sem_rows.at[slot]).start()
        return carry

    lax.fori_loop(0, tm, body, 0, unroll=8)

    @pl.when(i > 0)
    def _():
        rows_wait(i - 1, 1 - slot)

    @pl.when(i == n_tiles - 1)
    def _():
        rows_wait(i, slot)


def _dispatch(tok_tbl, xn, *, tm):
    G = tok_tbl.shape[0]
    D = xn.shape[1]
    kern = functools.partial(_dispatch_kernel, tm=tm, n_tiles=G)
    return pl.pallas_call(
        kern,
        grid=(G,),
        in_specs=[pl.BlockSpec(memory_space=pl.ANY), pl.BlockSpec(memory_space=pl.ANY)],
        out_specs=pl.BlockSpec(memory_space=pl.ANY),
        scratch_shapes=[
            pltpu.SMEM((2, tm), jnp.int32),
            pltpu.SemaphoreType.DMA((2,)),
            pltpu.SemaphoreType.DMA((2,)),
        ],
        out_shape=jax.ShapeDtypeStruct((G * tm, D), xn.dtype),
        compiler_params=pltpu.CompilerParams(dimension_semantics=("arbitrary",), has_side_effects=True),
        name="moe_dispatch",
    )(tok_tbl, xn)


def _expert_kernel(te_ref, nt_ref, tl_ref, xs_ref, wg_ref, wu_ref, wd_ref, o_ref, *, sbm):
    i = pl.program_id(0)
    f = pl.program_id(1)
    tm = xs_ref.shape[0]
    del nt_ref
    ln = tl_ref[i]
    for sb in range(tm // sbm):
        rows = slice(sb * sbm, (sb + 1) * sbm)

        @pl.when(sb * sbm < ln)
        def _(rows=rows):
            x = xs_ref[rows, :].astype(wg_ref.dtype)
            gate = jnp.dot(x, wg_ref[...], preferred_element_type=F32)
            up = jnp.dot(x, wu_ref[...], preferred_element_type=F32)
            h = (jax.nn.silu(gate) * up).astype(wd_ref.dtype)
            y = jnp.dot(h, wd_ref[...], preferred_element_type=F32)

            @pl.when(f == 0)
            def _():
                o_ref[rows, :] = y

            @pl.when(f > 0)
            def _():
                o_ref[rows, :] += y

        @pl.when(jnp.logical_and(sb * sbm >= ln, f == 0))
        def _(rows=rows):
            o_ref[rows, :] = jnp.zeros((sbm, o_ref.shape[1]), o_ref.dtype)


def _experts(te, n_tiles, tlen, xs, wg_bf, wu_bf, wd_bf, *, tm):
    G = te.shape[0]
    E, D, dff = wg_bf.shape
    tf = min(256, dff)
    F = dff // tf
    sbm = min(256, tm)
    est = 2 * tm * D * 4 + 2 * 3 * D * tf * 2 + 2 * tm * D * 4 + 6 * sbm * tf * 4 + 3 * sbm * D * 4

    def tile_ix(i, nt):
        return jnp.minimum(i, nt[0] - 1)

    def f_ix(i, f, nt):
        return jnp.where(i < nt[0], f, F - 1)

    kern = functools.partial(_expert_kernel, sbm=sbm)
    return pl.pallas_call(
        kern,
        grid_spec=pltpu.PrefetchScalarGridSpec(
            num_scalar_prefetch=3,
            grid=(G, F),
            in_specs=[
                pl.BlockSpec((tm, D), lambda i, f, te, nt, tl: (tile_ix(i, nt), 0)),
                pl.BlockSpec((None, D, tf), lambda i, f, te, nt, tl: (te[i], 0, f_ix(i, f, nt))),
                pl.BlockSpec((None, D, tf), lambda i, f, te, nt, tl: (te[i], 0, f_ix(i, f, nt))),
                pl.BlockSpec((None, tf, D), lambda i, f, te, nt, tl: (te[i], f_ix(i, f, nt), 0)),
            ],
            out_specs=pl.BlockSpec((tm, D), lambda i, f, te, nt, tl: (i, 0)),
        ),
        out_shape=jax.ShapeDtypeStruct((G * tm, D), F32),
        compiler_params=_params(("arbitrary", "arbitrary"), est),
        name="moe_experts",
    )(te, n_tiles, tlen, xs, wg_bf, wu_bf, wd_bf)


def _combine_kernel(pos_ref, ys_ref, x1_ref, gate_ref, o_ref, idx_ref, ybuf_ref, sem_idx, sem_rows, *,
                    tile_off, n_tiles, tt):
    i = pl.program_id(0)
    slot = i % 2

    def idx_copy(tile, s):
        return pltpu.make_async_copy(pos_ref.at[tile + tile_off], idx_ref.at[s], sem_idx.at[s])

    def issue_rows(s):
        def body(r, carry):
            row = idx_ref[s, r]
            pltpu.make_async_copy(ys_ref.at[row], ybuf_ref.at[s, r], sem_rows.at[s]).start()
            return carry

        lax.fori_loop(0, TOP_K * tt, body, 0, unroll=8)

    @pl.when(i == 0)
    def _():
        first = idx_copy(0, 0)
        first.start()
        first.wait()
        issue_rows(0)
        if n_tiles > 1:
            idx_copy(1, 1).start()

    @pl.when(i + 1 < n_tiles)
    def _():
        idx_copy(i + 1, 1 - slot).wait()
        issue_rows(1 - slot)

        @pl.when(i + 2 < n_tiles)
        def _():
            idx_copy(i + 2, slot).start()

    buf = ybuf_ref.at[slot]
    pltpu.make_async_copy(buf, buf, sem_rows.at[slot]).wait()
    g = gate_ref[...]
    y0 = ybuf_ref[slot, 0:tt, :]
    y1 = ybuf_ref[slot, tt:2 * tt, :]
    o_ref[...] = x1_ref[...] + (y0 * g[:, 0:1] + y1 * g[:, 1:2])


def _combine(pos_tbl, ys, x1, gate, *, tile_off, tt):
    T, D = x1.shape
    n_tiles = T // tt
    est = 2 * TOP_K * tt * D * 4 + 2 * 2 * tt * D * 4 + 2 * tt * LANES * 4 + 3 * tt * D * 4
    kern = functools.partial(_combine_kernel, tile_off=tile_off, n_tiles=n_tiles, tt=tt)
    return pl.pallas_call(
        kern,
        grid=(n_tiles,),
        in_specs=[
            pl.BlockSpec(memory_space=pl.ANY),
            pl.BlockSpec(memory_space=pl.ANY),
            pl.BlockSpec((tt, D), lambda i: (i, 0)),
            pl.BlockSpec((tt, LANES), lambda i: (i + tile_off, 0)),
        ],
        out_specs=pl.BlockSpec((tt, D), lambda i: (i, 0)),
        out_shape=jax.ShapeDtypeStruct((T, D), F32),
        scratch_shapes=[
            pltpu.SMEM((2, TOP_K * tt), jnp.int32),
            pltpu.VMEM((2, TOP_K * tt, D), F32),
            pltpu.SemaphoreType.DMA((2,)),
            pltpu.SemaphoreType.DMA((2,)),
        ],
        compiler_params=_params(("arbitrary",), est),
        name="moe_combine",
    )(pos_tbl, ys, x1, gate)


def _rope_tables(seq, hd, rot_dim):
    half = rot_dim // 2
    inv_freq = ROPE_THETA ** (-jnp.arange(half, dtype=F32) * 2.0 / rot_dim)
    ang = jnp.arange(seq, dtype=F32)[:, None] * inv_freq[None, :]
    cos, sin = jnp.cos(ang), jnp.sin(ang)
    pad = hd - rot_dim
    cos_t = jnp.concatenate([cos, cos, jnp.ones((seq, pad), F32)], axis=1)
    zeros_h = jnp.zeros((seq, half), F32)
    zeros_p = jnp.zeros((seq, pad), F32)
    sa_t = jnp.concatenate([zeros_h, sin, zeros_p], axis=1)
    sb_t = jnp.concatenate([-sin, zeros_h, zeros_p], axis=1)
    return cos_t, sa_t, sb_t


def kernel(x_prompt, x_sample, norm_mix, w_in, q_norm, k_norm, lambda_q1, lambda_k1, lambda_q2,
           lambda_k2, subln, w_pool, pool_scale, w_out, norm_ffn, w_router_group, b_router_group,
           w_router_expert, b_router_expert, w_gate, w_up, w_down):
    n_layers, D, in_cols = w_in.shape
    mix = w_out.shape[1]
    aw = (in_cols - mix) // 2
    pw = mix - aw
    hd = q_norm.shape[-1]
    rot_dim = hd // 4
    n_groups = w_router_group.shape[-1]
    n_experts = w_router_expert.shape[-1]
    assert aw % (2 * hd) == 0 and pw % len(POOL_WINDOWS) == 0
    assert n_groups + n_experts <= LANES and (n_experts // n_groups) & (n_experts // n_groups - 1) == 0

    trunks = []
    for x in (x_prompt, x_sample):
        b, s, _ = x.shape
        trunks.append(dict(batch=b, seq=s, x=x.reshape(b * s, D), rope=_rope_tables(s, hd, rot_dim),
                           bands=_pool_bands(min(s, 512))))
    t_a = trunks[0]["x"].shape[0]
    t_b = trunks[1]["x"].shape[0]
    tm_moe = min(512, t_a, t_b)
    tt_cmb = min(256, t_a, t_b)

    for l in range(n_layers):
        lam_init = _lambda_init(l)
        scale = hd ** -0.5
        w_in_bf = w_in[l].astype(BF16)
        head_gain = jnp.concatenate([jnp.tile(q_norm[l] * scale, aw // hd),
                                     jnp.tile(k_norm[l], aw // hd)]).reshape(1, 2 * aw)
        lamv = jnp.stack([lambda_q1[l], lambda_k1[l], lambda_q2[l], lambda_k2[l]]).astype(F32)
        w_pool_bf = w_pool[l].astype(BF16)
        wa_bf = w_out[l, :aw].astype(BF16)
        wp_bf = w_out[l, aw:].astype(BF16)
        w_r = jnp.concatenate([w_router_group[l], w_router_expert[l]], axis=1)
        w_r = jnp.pad(w_r, ((0, 0), (0, LANES - w_r.shape[1])))
        w_r_hi = w_r.astype(BF16)
        w_r_lo = (w_r - w_r_hi.astype(F32)).astype(BF16)
        b_r = jnp.pad(jnp.concatenate([b_router_group[l], b_router_expert[l]]).astype(F32),
                      (0, LANES - n_groups - n_experts)).reshape(1, LANES)
        wg_bf = w_gate[l].astype(BF16)
        wu_bf = w_up[l].astype(BF16)
        wd_bf = w_down[l].astype(BF16)

        x1s = []
        for tr in trunks:
            cos_t, sa_t, sb_t = tr["rope"]
            proj = _inproj(tr["x"], norm_mix[l].reshape(1, D), w_in_bf, head_gain, cos_t, sa_t, sb_t,
                           seq=tr["seq"], aw=aw, hd=hd, rot_half=rot_dim // 2)
            a = _attention(proj, lamv, subln[l].reshape(1, 2 * hd), batch=tr["batch"], seq=tr["seq"],
                           aw=aw, hd=hd, lam_init=lam_init)
            p = _pool(proj, w_pool_bf, pool_scale[l].reshape(1, pw), tr["bands"], batch=tr["batch"],
                      seq=tr["seq"], aw=aw, pw=pw)
            x1s.append(_outproj(a, p, wa_bf, wp_bf, tr["x"]))

        xn, eid, gate = _router(x1s[0], x1s[1], norm_ffn[l].reshape(1, D), w_r_hi, w_r_lo, b_r,
                                n_groups=n_groups, n_experts=n_experts)
        te, tlen, n_tiles, tok_tbl, pos, _ = _moe_plan(eid[:, :TOP_K], tm=tm_moe, n_experts=n_experts)
        xs = _dispatch(tok_tbl, xn, tm=tm_moe)
        ys = _experts(te, n_tiles, tlen, xs, wg_bf, wu_bf, wd_bf, tm=tm_moe)
        t_all = t_a + t_b
        pos_tbl = pos.reshape(t_all // tt_cmb, tt_cmb, TOP_K).transpose(0, 2, 1).reshape(
            t_all // tt_cmb, TOP_K * tt_cmb)
        outs = []
        tile_off = 0
        for tr, x1 in zip(trunks, x1s):
            outs.append(_combine(pos_tbl, ys, x1, gate, tile_off=tile_off, tt=tt_cmb))
            tile_off += x1.shape[0] // tt_cmb
        for tr, o in zip(trunks, outs):
            tr["x"] = o

    return tuple(tr["x"].reshape(tr["batch"], tr["seq"], D) for tr in trunks)
```

```python
import functools
import math

import jax
import jax.numpy as jnp
import numpy as np
from jax import lax
from jax.experimental import pallas as pl
from jax.experimental.pallas import tpu as pltpu

EPS = 1e-6
ROPE_THETA = 500000.0
POOL_WINDOWS = (2, 4, 8, 16)
TOP_K = 2
POOL_HALO = 16
LANES = 128
VMEM_CAP_BYTES = 60000 * 1024
F32 = jnp.float32
BF16 = jnp.bfloat16


def _lambda_init(layer_idx):
    return 0.8 - 0.6 * math.exp(-0.3 * layer_idx)


def _vmem_limit(est_bytes):
    return int(min(VMEM_CAP_BYTES, est_bytes * 5 // 4 + (4 << 20)))


def _params(sem, est_bytes):
    return pltpu.CompilerParams(dimension_semantics=sem, vmem_limit_bytes=_vmem_limit(est_bytes))


def _inproj_kernel(x_ref, g_ref, w_ref, hg_ref, cos_ref, sa_ref, sb_ref, o_ref, xn_ref, *,
                   n_qk_tiles, hd, rot_half):
    j = pl.program_id(1)

    @pl.when(j == 0)
    def _():
        x = x_ref[...]
        ms = jnp.mean(x * x, axis=-1, keepdims=True)
        xn_ref[...] = (x * lax.rsqrt(ms + EPS) * g_ref[...]).astype(xn_ref.dtype)

    acc = jnp.dot(xn_ref[...], w_ref[...], preferred_element_type=F32)
    tn = acc.shape[1]

    @pl.when(j < n_qk_tiles)
    def _():
        cos = cos_ref[...]
        sa = sa_ref[...]
        sb = sb_ref[...]
        for c in range(tn // hd):
            blk = acc[:, c * hd:(c + 1) * hd]
            ms = jnp.mean(blk * blk, axis=-1, keepdims=True)
            y = blk * lax.rsqrt(ms + EPS) * hg_ref[:, c * hd:(c + 1) * hd]
            y = (y * cos + pltpu.roll(y, rot_half, 1) * sa
                 + pltpu.roll(y, hd - rot_half, 1) * sb)
            o_ref[:, c * hd:(c + 1) * hd] = y.astype(o_ref.dtype)

    @pl.when(j >= n_qk_tiles)
    def _():
        o_ref[...] = acc.astype(o_ref.dtype)


def _inproj(x2, gain, w_bf, head_gain, cos_t, sa_t, sb_t, *, seq, aw, hd, rot_half):
    T, D = x2.shape
    N = w_bf.shape[1]
    tm = min(512, seq)
    tn = min(1024, 2 * aw)
    n_qk_tiles = (2 * aw) // tn
    s_tiles = seq // tm
    est = 2 * tm * D * 4 + 2 * D * tn * 2 + tm * D * 2 + 2 * tm * tn * 2 + 2 * tm * tn * 4
    kern = functools.partial(_inproj_kernel, n_qk_tiles=n_qk_tiles, hd=hd, rot_half=rot_half)
    return pl.pallas_call(
        kern,
        grid=(T // tm, N // tn),
        in_specs=[
            pl.BlockSpec((tm, D), lambda i, j: (i, 0)),
            pl.BlockSpec((1, D), lambda i, j: (0, 0)),
            pl.BlockSpec((D, tn), lambda i, j: (0, j)),
            pl.BlockSpec((1, tn), lambda i, j: (0, jnp.minimum(j, n_qk_tiles - 1))),
            pl.BlockSpec((tm, hd), lambda i, j: (i % s_tiles, 0)),
            pl.BlockSpec((tm, hd), lambda i, j: (i % s_tiles, 0)),
            pl.BlockSpec((tm, hd), lambda i, j: (i % s_tiles, 0)),
        ],
        out_specs=pl.BlockSpec((tm, tn), lambda i, j: (i, j)),
        out_shape=jax.ShapeDtypeStruct((T, N), BF16),
        scratch_shapes=[pltpu.VMEM((tm, D), BF16)],
        compiler_params=_params(("arbitrary", "arbitrary"), est),
        name="inproj",
    )(x2, gain, w_bf, head_gain, cos_t, sa_t, sb_t)


def _attn_kernel(lam_ref, subln_ref, q_ref, k_ref, v_ref, o_ref,
                 vt_ref, e_ref, mc_ref, lc_ref, acc_ref, *, tk, sub, hd, lam_init, unroll):
    seq = k_ref.shape[0]
    nck = seq // tk
    contract_last = (((1,), (1,)), ((), ()))

    @pl.when(pl.program_id(2) == 0)
    def _():
        def transpose_v(c, carry):
            vc = v_ref[pl.ds(pl.multiple_of(c * tk, tk), tk), :]
            vt_ref[c] = vc.astype(F32).T.astype(vt_ref.dtype)
            return carry

        lax.fori_loop(0, nck, transpose_v, 0)

    q = q_ref[...]
    nsub = tk // sub

    def pass1(c, carry):
        kc = k_ref[pl.ds(pl.multiple_of(c * tk, tk), tk), :]
        for comp in range(2):
            st = lax.dot_general(kc[:, comp * hd:(comp + 1) * hd], q[:, comp * hd:(comp + 1) * hd],
                                 contract_last, preferred_element_type=F32)
            for j in range(nsub):
                sj = st[j * sub:(j + 1) * sub]
                mj = jnp.max(sj, axis=0, keepdims=True)
                ej = jnp.exp2(sj - mj)
                mc_ref[comp, c, j:j + 1, :] = mj
                lc_ref[comp, c, j:j + 1, :] = jnp.sum(ej, axis=0, keepdims=True)
                e_ref[comp, c, j * sub:(j + 1) * sub, :] = ej
        return carry

    lax.fori_loop(0, nck, pass1, 0, unroll=unroll)

    lv = lam_ref[...]
    lam = (jnp.exp(jnp.sum(lv[0:1] * lv[1:2], axis=1, keepdims=True))
           - jnp.exp(jnp.sum(lv[2:3] * lv[3:4], axis=1, keepdims=True)) + lam_init)
    def over_blocks(red, x):
        return red(red(x, axis=0), axis=0, keepdims=True)

    m0 = over_blocks(jnp.max, mc_ref[0])
    m1 = over_blocks(jnp.max, mc_ref[1])
    r0 = 1.0 / over_blocks(jnp.sum, lc_ref[0] * jnp.exp2(mc_ref[0] - m0))
    r1 = lam / over_blocks(jnp.sum, lc_ref[1] * jnp.exp2(mc_ref[1] - m1))
    acc_ref[...] = jnp.zeros(acc_ref.shape, F32)

    def pass2(c, carry):
        f0 = jnp.exp2(mc_ref[0, c] - m0) * r0
        f1 = jnp.exp2(mc_ref[1, c] - m1) * r1
        wt = jnp.concatenate(
            [e_ref[0, c, j * sub:(j + 1) * sub, :] * f0[j:j + 1]
             - e_ref[1, c, j * sub:(j + 1) * sub, :] * f1[j:j + 1] for j in range(nsub)], axis=0)
        acc_ref[...] += jnp.dot(vt_ref[c], wt.astype(vt_ref.dtype), preferred_element_type=F32)
        return carry

    lax.fori_loop(0, nck, pass2, 0, unroll=unroll)

    o = acc_ref[...].T
    ms = jnp.mean(o * o, axis=-1, keepdims=True)
    o = o * lax.rsqrt(ms + EPS) * subln_ref[...]
    o_ref[...] = (o * (1.0 - lam_init)).astype(o_ref.dtype)


def _attention(proj, lamv, subln, *, batch, seq, aw, hd, lam_init):
    T = proj.shape[0]
    vd = 2 * hd
    n_heads = aw // vd
    tq = min(seq, 512 if seq <= 2048 else 256)
    tk = min(seq, 512)
    nck = seq // tk
    q_tiles = seq // tq
    k_col0 = aw // vd
    v_col0 = 2 * aw // vd
    est = (2 * 2 * seq * vd * 2 + seq * vd * 2 + 2 * seq * tq * 4 + 2 * (nck + 2) * 8 * tq * 4
           + tq * vd * 4 + 4 * tq * vd * 2 + 6 * tq * tk * 4)
    sub = min(tk, max(8, (32 * 8 * LANES) // tq))
    nsub = tk // sub
    kern = functools.partial(_attn_kernel, tk=tk, sub=sub, hd=hd, lam_init=lam_init, unroll=min(8, nck))
    return pl.pallas_call(
        kern,
        grid=(batch, n_heads, q_tiles),
        in_specs=[
            pl.BlockSpec((4, hd), lambda b, h, i: (0, 0)),
            pl.BlockSpec((1, vd), lambda b, h, i: (0, 0)),
            pl.BlockSpec((tq, vd), lambda b, h, i: (b * q_tiles + i, h)),
            pl.BlockSpec((seq, vd), lambda b, h, i: (b, k_col0 + h)),
            pl.BlockSpec((seq, vd), lambda b, h, i: (b, v_col0 + h)),
        ],
        out_specs=pl.BlockSpec((tq, vd), lambda b, h, i: (b * q_tiles + i, h)),
        out_shape=jax.ShapeDtypeStruct((T, aw), BF16),
        scratch_shapes=[
            pltpu.VMEM((nck, vd, tk), BF16),
            pltpu.VMEM((2, nck, tk, tq), F32),
            pltpu.VMEM((2, nck, nsub, tq), F32),
            pltpu.VMEM((2, nck, nsub, tq), F32),
            pltpu.VMEM((vd, tq), F32),
        ],
        compiler_params=_params(("arbitrary", "arbitrary", "arbitrary"), est),
        name="diff_attention",
    )(lamv, subln, proj, proj, proj)


def _pool_bands(rows):
    t = np.arange(rows)[:, None]
    jm = np.arange(rows)[None, :]
    jh = np.arange(POOL_HALO)[None, :]
    bm, bp, bn = [], [], []
    for w in POOL_WINDOWS:
        left = w // 2
        right = w - 1 - left
        bm.append((jm >= t - left) & (jm <= t + right))
        bp.append(jh - POOL_HALO >= t - left)
        bn.append(rows + jh <= t + right)
    as_bf = lambda m: jnp.asarray(np.stack(m).astype(np.float32), dtype=BF16)
    return as_bf(bm), as_bf(bp), as_bf(bn)


def _pool_kernel(u_ref, wp_ref, ps_ref, bm_ref, bp_ref, bn_ref, o_ref, *, rows):
    g = pl.program_id(1)
    seq = u_ref.shape[0]
    n_chunks = seq // rows
    for gi, w in enumerate(POOL_WINDOWS):
        left = w // 2
        right = w - 1 - left

        @pl.when(g == gi)
        def _(gi=gi, left=left, right=right):
            def body(r, carry):
                r0 = pl.multiple_of(r * rows, rows)
                main = u_ref[pl.ds(r0, rows), :]
                p0 = pl.multiple_of(jnp.maximum(r0 - POOL_HALO, 0), POOL_HALO)
                n0 = pl.multiple_of(jnp.minimum(r0 + rows, seq - POOL_HALO), POOL_HALO)
                prev = u_ref[pl.ds(p0, POOL_HALO), :]
                nxt = u_ref[pl.ds(n0, POOL_HALO), :]
                prev = jnp.where(r > 0, prev, jnp.zeros_like(prev))
                nxt = jnp.where(r < n_chunks - 1, nxt, jnp.zeros_like(nxt))
                ssum = (jnp.dot(bm_ref[gi], main, preferred_element_type=F32)
                        + jnp.dot(bp_ref[gi], prev, preferred_element_type=F32)
                        + jnp.dot(bn_ref[gi], nxt, preferred_element_type=F32))
                t = r0 + lax.broadcasted_iota(jnp.int32, (rows, 1), 0)
                lo = jnp.maximum(t - left, 0)
                hi = jnp.minimum(t + right, seq - 1)
                cnt = (hi - lo + 1).astype(F32)
                delta = ssum / cnt - main.astype(F32)
                y = jnp.dot(delta.astype(wp_ref.dtype), wp_ref[...], preferred_element_type=F32)
                o_ref[pl.ds(r0, rows), :] = (y * ps_ref[...]).astype(o_ref.dtype)
                return carry

            lax.fori_loop(0, n_chunks, body, 0)


def _pool(proj, w_pool_bf, pool_scale, bands, *, batch, seq, aw, pw):
    T = proj.shape[0]
    n_groups, gw, _ = w_pool_bf.shape
    rows = min(seq, 512)
    u_col0 = 3 * aw // gw
    bm, bp, bn = bands
    est = (2 * 2 * seq * gw * 2 + 2 * gw * gw * 2 + 2 * 4 * rows * (rows + 2 * POOL_HALO) * 2
           + 8 * rows * gw * 4)
    kern = functools.partial(_pool_kernel, rows=rows)
    return pl.pallas_call(
        kern,
        grid=(batch, n_groups),
        in_specs=[
            pl.BlockSpec((seq, gw), lambda b, g: (b, u_col0 + g)),
            pl.BlockSpec((None, gw, gw), lambda b, g: (g, 0, 0)),
            pl.BlockSpec((1, gw), lambda b, g: (0, g)),
            pl.BlockSpec(bm.shape, lambda b, g: (0, 0, 0)),
            pl.BlockSpec(bp.shape, lambda b, g: (0, 0, 0)),
            pl.BlockSpec(bn.shape, lambda b, g: (0, 0, 0)),
        ],
        out_specs=pl.BlockSpec((seq, gw), lambda b, g: (b, g)),
        out_shape=jax.ShapeDtypeStruct((T, pw), BF16),
        compiler_params=_params(("arbitrary", "arbitrary"), est),
        name="multiscale_pool",
    )(proj, w_pool_bf, pool_scale, bm, bp, bn)


def _outproj_kernel(a_ref, p_ref, wa_ref, wp_ref, x_ref, o_ref):
    acc = (jnp.dot(a_ref[...], wa_ref[...], preferred_element_type=F32)
           + jnp.dot(p_ref[...], wp_ref[...], preferred_element_type=F32))
    o_ref[...] = x_ref[...] + acc


def _outproj(a, p, wa_bf, wp_bf, x2):
    T, D = x2.shape
    aw = a.shape[1]
    pw = p.shape[1]
    tm = min(512, T)
    tn = min(1024, D)
    est = 2 * (tm * aw * 2 + tm * pw * 2 + (aw + pw) * tn * 2 + 2 * tm * tn * 4) + 2 * tm * tn * 4
    return pl.pallas_call(
        _outproj_kernel,
        grid=(T // tm, D // tn),
        in_specs=[
            pl.BlockSpec((tm, aw), lambda i, j: (i, 0)),
            pl.BlockSpec((tm, pw), lambda i, j: (i, 0)),
            pl.BlockSpec((aw, tn), lambda i, j: (0, j)),
            pl.BlockSpec((pw, tn), lambda i, j: (0, j)),
            pl.BlockSpec((tm, tn), lambda i, j: (i, j)),
        ],
        out_specs=pl.BlockSpec((tm, tn), lambda i, j: (i, j)),
        out_shape=jax.ShapeDtypeStruct((T, D), F32),
        compiler_params=_params(("arbitrary", "arbitrary"), est),
        name="outproj",
    )(a, p, wa_bf, wp_bf, x2)


def _router_kernel(xa_ref, xb_ref, g_ref, whi_ref, wlo_ref, b_ref, xn_ref, eid_ref, gate_ref, *,
                   a_tiles, n_groups, n_experts):
    i = pl.program_id(0)
    epg_shift = int(math.log2(n_experts // n_groups))

    def run(x_ref):
        x = x_ref[...]
        ms = jnp.mean(x * x, axis=-1, keepdims=True)
        xn = x * lax.rsqrt(ms + EPS) * g_ref[...]
        hi = xn.astype(BF16)
        xn_ref[...] = hi.astype(F32)
        lo = (xn - hi.astype(F32)).astype(BF16)
        whi = whi_ref[...]
        logits = (jnp.dot(hi, whi, preferred_element_type=F32)
                  + jnp.dot(lo, whi, preferred_element_type=F32)
                  + jnp.dot(hi, wlo_ref[...], preferred_element_type=F32)
                  + b_ref[...])
        lane = lax.broadcasted_iota(jnp.int32, logits.shape, 1)
        neg = jnp.float32(-jnp.inf)
        is_g = lane < n_groups
        gl = jnp.where(is_g, logits, neg)
        gmax = jnp.max(gl, axis=1, keepdims=True)
        g_idx = jnp.min(jnp.where(gl == gmax, lane, LANES), axis=1, keepdims=True)
        gsum = jnp.sum(jnp.where(is_g, jnp.exp(gl - gmax), 0.0), axis=1, keepdims=True)
        g_w = 1.0 / gsum
        e_lane = lane - n_groups
        grp = jnp.where(jnp.logical_and(e_lane >= 0, e_lane < n_experts),
                        lax.shift_right_arithmetic(e_lane, epg_shift), -1)
        sel = grp == g_idx
        el = jnp.where(sel, logits, neg)
        emax = jnp.max(el, axis=1, keepdims=True)
        ex = jnp.where(sel, jnp.exp(el - emax), 0.0)
        prob = ex / jnp.sum(ex, axis=1, keepdims=True)
        pm = jnp.where(sel, prob, -1.0)
        p1 = jnp.max(pm, axis=1, keepdims=True)
        i1 = jnp.min(jnp.where(pm == p1, lane, LANES), axis=1, keepdims=True)
        pm2 = jnp.where(lane == i1, -1.0, pm)
        p2 = jnp.max(pm2, axis=1, keepdims=True)
        i2 = jnp.min(jnp.where(pm2 == p2, lane, LANES), axis=1, keepdims=True)
        tsum = p1 + p2
        gate1 = g_w * (p1 / tsum)
        gate2 = g_w * (p2 / tsum)
        eid_ref[...] = jnp.where(lane == 0, i1 - n_groups, jnp.where(lane == 1, i2 - n_groups, 0))
        gate_ref[...] = jnp.where(lane == 0, gate1, jnp.where(lane == 1, gate2, 0.0))

    @pl.when(i < a_tiles)
    def _():
        run(xa_ref)

    @pl.when(i >= a_tiles)
    def _():
        run(xb_ref)


def _router(x1a, x1b, gain, whi, wlo, bias, *, n_groups, n_experts):
    Ta, D = x1a.shape
    Tb = x1b.shape[0]
    tt = min(256, Ta, Tb)
    a_tiles = Ta // tt
    b_tiles = Tb // tt
    T = Ta + Tb
    est = 2 * 2 * tt * D * 4 + 2 * tt * D * 4 + 4 * D * LANES * 2 + 6 * tt * D * 4
    kern = functools.partial(_router_kernel, a_tiles=a_tiles, n_groups=n_groups, n_experts=n_experts)
    return pl.pallas_call(
        kern,
        grid=(a_tiles + b_tiles,),
        in_specs=[
            pl.BlockSpec((tt, D), lambda i: (jnp.minimum(i, a_tiles - 1), 0)),
            pl.BlockSpec((tt, D), lambda i: (jnp.maximum(i - a_tiles, 0), 0)),
            pl.BlockSpec((1, D), lambda i: (0, 0)),
            pl.BlockSpec((D, LANES), lambda i: (0, 0)),
            pl.BlockSpec((D, LANES), lambda i: (0, 0)),
            pl.BlockSpec((1, LANES), lambda i: (0, 0)),
        ],
        out_specs=[
            pl.BlockSpec((tt, D), lambda i: (i, 0)),
            pl.BlockSpec((tt, LANES), lambda i: (i, 0)),
            pl.BlockSpec((tt, LANES), lambda i: (i, 0)),
        ],
        out_shape=[
            jax.ShapeDtypeStruct((T, D), F32),
            jax.ShapeDtypeStruct((T, LANES), jnp.int32),
            jax.ShapeDtypeStruct((T, LANES), F32),
        ],
        compiler_params=_params(("arbitrary",), est),
        name="router",
    )(x1a, x1b, gain, whi, wlo, bias)


def _moe_plan(eid2, *, tm, n_experts):
    T = eid2.shape[0]
    A = T * TOP_K
    eflat = eid2.reshape(A)
    order = jnp.argsort(eflat, stable=True).astype(jnp.int32)
    sorted_e = eflat[order]
    ends = jnp.searchsorted(sorted_e, jnp.arange(n_experts, dtype=jnp.int32), side="right").astype(jnp.int32)
    starts = jnp.concatenate([jnp.zeros((1,), jnp.int32), ends[:-1]])
    counts = ends - starts
    ntile_e = (counts + tm - 1) // tm
    tile_end = jnp.cumsum(ntile_e).astype(jnp.int32)
    tile_base = tile_end - ntile_e
    n_tiles = tile_end[-1]
    G = -(-A // tm) + n_experts
    ti = jnp.arange(G, dtype=jnp.int32)
    valid = ti < n_tiles
    te = jnp.clip(jnp.searchsorted(tile_end, ti, side="right"), 0, n_experts - 1).astype(jnp.int32)
    te = jnp.where(valid, te, te[n_tiles - 1])
    j = ti - tile_base[te]
    tstart = starts[te] + j * tm
    tlen = jnp.where(valid, jnp.clip(counts[te] - j * tm, 0, tm), 0).astype(jnp.int32)
    r = jnp.arange(tm, dtype=jnp.int32)
    sidx = jnp.clip(tstart[:, None] + r[None, :], 0, A - 1)
    tok_tbl = jnp.where(r[None, :] < tlen[:, None], order[sidx] // TOP_K, 0).astype(jnp.int32)
    rank = jnp.arange(A, dtype=jnp.int32) - starts[sorted_e]
    row = tile_base[sorted_e] * tm + rank
    pos = jnp.zeros((A,), jnp.int32).at[order].set(row).reshape(T, TOP_K)
    return te, tlen, n_tiles.reshape(1), tok_tbl, pos, G


def _expert_kernel(te_ref, nt_ref, tl_ref, tbl_ref, xn_ref, wg_ref, wu_ref, wd_ref, o_ref,
                   idx_ref, xrow_ref, xb_ref, h_ref, sem_idx, sem_rows, *,
                   n_tiles, fa, tf, half, grp):
    del te_ref, nt_ref
    i = pl.program_id(0)
    s = pl.program_id(1)
    tm = xrow_ref.shape[0]
    n_grp = tm // grp
    ln = tl_ref[i]

    def idx_copy(tile):
        dst = idx_ref.at[pl.ds(pl.multiple_of((tile % 2) * tm, tm), tm)]
        return pltpu.make_async_copy(tbl_ref.at[tile], dst, sem_idx.at[tile % 2])

    def issue_rows(tile):
        cnt = tl_ref[tile]
        base = (tile % 2) * tm
        for g in range(n_grp):
            @pl.when(g * grp < cnt)
            def _(g=g):
                def body(r8, carry):
                    row0 = pl.multiple_of(g * grp + r8 * 8, 8)
                    for j in range(8):
                        tok = idx_ref[base + row0 + j]
                        pltpu.make_async_copy(xn_ref.at[tok], xrow_ref.at[row0 + j], sem_rows).start()
                    return carry

                lax.fori_loop(0, grp // 8, body, 0)

    def wait_rows():
        for g in range(n_grp):
            @pl.when(g * grp < ln)
            def _(g=g):
                blk = xrow_ref.at[pl.ds(g * grp, grp)]
                pltpu.make_async_copy(blk, blk, sem_rows).wait()

    @pl.when(jnp.logical_and(i == 0, s == 0))
    def _():
        xrow_ref[...] = jnp.zeros(xrow_ref.shape, xrow_ref.dtype)
        first = idx_copy(0)
        first.start()
        first.wait()
        issue_rows(0)
        if n_tiles > 1:
            idx_copy(1).start()

    @pl.when(s == 0)
    def _():
        wait_rows()
        xb_ref[...] = xrow_ref[...].astype(xb_ref.dtype)

    @pl.when(jnp.logical_and(s == 1, i + 1 < n_tiles))
    def _():
        idx_copy(i + 1).wait()
        issue_rows(i + 1)

        @pl.when(i + 2 < n_tiles)
        def _():
            idx_copy(i + 2).start()

    def rows_variants(fn):
        @pl.when(ln > half)
        def _():
            fn(tm)

        @pl.when(jnp.logical_and(ln > 0, ln <= half))
        def _():
            fn(half)

    for f in range(fa):
        @pl.when(s == f)
        def _(f=f):
            def gate_up(nrows):
                x = xb_ref[0:nrows, :]
                gate = jnp.dot(x, wg_ref[...], preferred_element_type=F32)
                up = jnp.dot(x, wu_ref[...], preferred_element_type=F32)
                h_ref[0:nrows, f * tf:(f + 1) * tf] = (jax.nn.silu(gate) * up).astype(h_ref.dtype)

            rows_variants(gate_up)

    @pl.when(s >= fa)
    def _():
        def down(nrows):
            o_ref[0:nrows, :] = jnp.dot(h_ref[0:nrows, :], wd_ref[...], preferred_element_type=F32)

        rows_variants(down)

        @pl.when(ln <= half)
        def _():
            o_ref[half:tm, :] = jnp.zeros((tm - half, o_ref.shape[1]), o_ref.dtype)

        @pl.when(ln == 0)
        def _():
            o_ref[0:half, :] = jnp.zeros((half, o_ref.shape[1]), o_ref.dtype)


def _experts(te, n_tiles, tlen, tok_tbl, xn, wg_bf, wu_bf, wd_bf, *, tm):
    G = te.shape[0]
    E, D, dff = wg_bf.shape
    tf = min(512, dff)
    fa = dff // tf
    tn = min(1024, D)
    fb = D // tn
    half = tm // 2
    grp = min(64, tm)
    assert fa >= 2, "the next tile's gather is issued at inner step 1"
    est = (tm * D * 4 + tm * D * 2 + tm * dff * 2 + 2 * 2 * D * tf * 2 + 2 * dff * tn * 2
           + 2 * tm * tn * 4 + 3 * tm * tf * 4 + tm * tn * 4)

    def a_ix(i, s, nt):
        return jnp.where(i < nt[0], jnp.minimum(s, fa - 1), fa - 1)

    def b_ix(s):
        return jnp.clip(s - fa, 0, fb - 1)

    def wd_ix(i, s, nt):
        return jnp.where(i < nt[0], b_ix(s), fb - 1)

    kern = functools.partial(_expert_kernel, n_tiles=G, fa=fa, tf=tf, half=half, grp=grp)
    return pl.pallas_call(
        kern,
        grid_spec=pltpu.PrefetchScalarGridSpec(
            num_scalar_prefetch=3,
            grid=(G, fa + fb),
            in_specs=[
                pl.BlockSpec(memory_space=pl.ANY),
                pl.BlockSpec(memory_space=pl.ANY),
                pl.BlockSpec((None, D, tf), lambda i, s, te, nt, tl: (te[i], 0, a_ix(i, s, nt))),
                pl.BlockSpec((None, D, tf), lambda i, s, te, nt, tl: (te[i], 0, a_ix(i, s, nt))),
                pl.BlockSpec((None, dff, tn), lambda i, s, te, nt, tl: (te[i], 0, wd_ix(i, s, nt))),
            ],
            out_specs=pl.BlockSpec((tm, tn), lambda i, s, te, nt, tl: (i, b_ix(s))),
            scratch_shapes=[
                pltpu.SMEM((2 * tm,), jnp.int32),
                pltpu.VMEM((tm, D), F32),
                pltpu.VMEM((tm, D), BF16),
                pltpu.VMEM((tm, dff), BF16),
                pltpu.SemaphoreType.DMA((2,)),
                pltpu.SemaphoreType.DMA(()),
            ],
        ),
        out_shape=jax.ShapeDtypeStruct((G * tm, D), F32),
        compiler_params=_params(("arbitrary", "arbitrary"), est),
        name="moe_experts",
    )(te, n_tiles, tlen, tok_tbl, xn, wg_bf, wu_bf, wd_bf)


def _combine_kernel(pos_ref, ys_ref, x1_ref, gate_ref, o_ref, idx_ref, ybuf_ref, sem_idx, sem_rows, *,
                    tile_off, n_tiles, tt):
    i = pl.program_id(0)
    slot = i % 2

    def idx_copy(tile, s):
        return pltpu.make_async_copy(pos_ref.at[tile + tile_off], idx_ref.at[s], sem_idx.at[s])

    def issue_rows(s):
        def body(r, carry):
            row = idx_ref[s, r]
            pltpu.make_async_copy(ys_ref.at[row], ybuf_ref.at[s, r], sem_rows.at[s]).start()
            return carry

        lax.fori_loop(0, TOP_K * tt, body, 0, unroll=8)

    @pl.when(i == 0)
    def _():
        first = idx_copy(0, 0)
        first.start()
        first.wait()
        issue_rows(0)
        if n_tiles > 1:
            idx_copy(1, 1).start()

    @pl.when(i + 1 < n_tiles)
    def _():
        idx_copy(i + 1, 1 - slot).wait()
        issue_rows(1 - slot)

        @pl.when(i + 2 < n_tiles)
        def _():
            idx_copy(i + 2, slot).start()

    buf = ybuf_ref.at[slot]
    pltpu.make_async_copy(buf, buf, sem_rows.at[slot]).wait()
    g = gate_ref[...]
    y0 = ybuf_ref[slot, 0:tt, :]
    y1 = ybuf_ref[slot, tt:2 * tt, :]
    o_ref[...] = x1_ref[...] + (y0 * g[:, 0:1] + y1 * g[:, 1:2])


def _combine(pos_tbl, ys, x1, gate, *, tile_off, tt):
    T, D = x1.shape
    n_tiles = T // tt
    est = 2 * TOP_K * tt * D * 4 + 2 * 2 * tt * D * 4 + 2 * tt * LANES * 4 + 3 * tt * D * 4
    kern = functools.partial(_combine_kernel, tile_off=tile_off, n_tiles=n_tiles, tt=tt)
    return pl.pallas_call(
        kern,
        grid=(n_tiles,),
        in_specs=[
            pl.BlockSpec(memory_space=pl.ANY),
            pl.BlockSpec(memory_space=pl.ANY),
            pl.BlockSpec((tt, D), lambda i: (i, 0)),
            pl.BlockSpec((tt, LANES), lambda i: (i + tile_off, 0)),
        ],
        out_specs=pl.BlockSpec((tt, D), lambda i: (i, 0)),
        out_shape=jax.ShapeDtypeStruct((T, D), F32),
        scratch_shapes=[
            pltpu.SMEM((2, TOP_K * tt), jnp.int32),
            pltpu.VMEM((2, TOP_K * tt, D), F32),
            pltpu.SemaphoreType.DMA((2,)),
            pltpu.SemaphoreType.DMA((2,)),
        ],
        compiler_params=_params(("arbitrary",), est),
        name="moe_combine",
    )(pos_tbl, ys, x1, gate)


def _rope_tables(seq, hd, rot_dim):
    half = rot_dim // 2
    inv_freq = ROPE_THETA ** (-jnp.arange(half, dtype=F32) * 2.0 / rot_dim)
    ang = jnp.arange(seq, dtype=F32)[:, None] * inv_freq[None, :]
    cos, sin = jnp.cos(ang), jnp.sin(ang)
    pad = hd - rot_dim
    cos_t = jnp.concatenate([cos, cos, jnp.ones((seq, pad), F32)], axis=1)
    zeros_h = jnp.zeros((seq, half), F32)
    zeros_p = jnp.zeros((seq, pad), F32)
    sa_t = jnp.concatenate([zeros_h, sin, zeros_p], axis=1)
    sb_t = jnp.concatenate([-sin, zeros_h, zeros_p], axis=1)
    return cos_t, sa_t, sb_t


def kernel(x_prompt, x_sample, norm_mix, w_in, q_norm, k_norm, lambda_q1, lambda_k1, lambda_q2,
           lambda_k2, subln, w_pool, pool_scale, w_out, norm_ffn, w_router_group, b_router_group,
           w_router_expert, b_router_expert, w_gate, w_up, w_down):
    n_layers, D, in_cols = w_in.shape
    mix = w_out.shape[1]
    aw = (in_cols - mix) // 2
    pw = mix - aw
    hd = q_norm.shape[-1]
    rot_dim = hd // 4
    n_groups = w_router_group.shape[-1]
    n_experts = w_router_expert.shape[-1]
    assert aw % (2 * hd) == 0 and pw % len(POOL_WINDOWS) == 0
    assert n_groups + n_experts <= LANES and (n_experts // n_groups) & (n_experts // n_groups - 1) == 0

    trunks = []
    for x in (x_prompt, x_sample):
        b, s, _ = x.shape
        trunks.append(dict(batch=b, seq=s, x=x.reshape(b * s, D), rope=_rope_tables(s, hd, rot_dim),
                           bands=_pool_bands(min(s, 512))))
    t_a = trunks[0]["x"].shape[0]
    t_b = trunks[1]["x"].shape[0]
    tm_moe = min(512, t_a, t_b)
    tt_cmb = min(256, t_a, t_b)

    for l in range(n_layers):
        lam_init = _lambda_init(l)
        scale = hd ** -0.5 * math.log2(math.e)
        w_in_bf = w_in[l].astype(BF16)
        head_gain = jnp.concatenate([jnp.tile(q_norm[l] * scale, aw // hd),
                                     jnp.tile(k_norm[l], aw // hd)]).reshape(1, 2 * aw)
        lamv = jnp.stack([lambda_q1[l], lambda_k1[l], lambda_q2[l], lambda_k2[l]]).astype(F32)
        w_pool_bf = w_pool[l].astype(BF16)
        wa_bf = w_out[l, :aw].astype(BF16)
        wp_bf = w_out[l, aw:].astype(BF16)
        w_r = jnp.concatenate([w_router_group[l], w_router_expert[l]], axis=1)
        w_r = jnp.pad(w_r, ((0, 0), (0, LANES - w_r.shape[1])))
        w_r_hi = w_r.astype(BF16)
        w_r_lo = (w_r - w_r_hi.astype(F32)).astype(BF16)
        b_r = jnp.pad(jnp.concatenate([b_router_group[l], b_router_expert[l]]).astype(F32),
                      (0, LANES - n_groups - n_experts)).reshape(1, LANES)
        wg_bf = w_gate[l].astype(BF16)
        wu_bf = w_up[l].astype(BF16)
        wd_bf = w_down[l].astype(BF16)

        x1s = []
        for tr in trunks:
            cos_t, sa_t, sb_t = tr["rope"]
            proj = _inproj(tr["x"], norm_mix[l].reshape(1, D), w_in_bf, head_gain, cos_t, sa_t, sb_t,
                           seq=tr["seq"], aw=aw, hd=hd, rot_half=rot_dim // 2)
            a = _attention(proj, lamv, subln[l].reshape(1, 2 * hd), batch=tr["batch"], seq=tr["seq"],
                           aw=aw, hd=hd, lam_init=lam_init)
            p = _pool(proj, w_pool_bf, pool_scale[l].reshape(1, pw), tr["bands"], batch=tr["batch"],
                      seq=tr["seq"], aw=aw, pw=pw)
            x1s.append(_outproj(a, p, wa_bf, wp_bf, tr["x"]))

        xn, eid, gate = _router(x1s[0], x1s[1], norm_ffn[l].reshape(1, D), w_r_hi, w_r_lo, b_r,
                                n_groups=n_groups, n_experts=n_experts)
        te, tlen, n_tiles, tok_tbl, pos, _ = _moe_plan(eid[:, :TOP_K], tm=tm_moe, n_experts=n_experts)
        ys = _experts(te, n_tiles, tlen, tok_tbl, xn, wg_bf, wu_bf, wd_bf, tm=tm_moe)
        t_all = t_a + t_b
        pos_tbl = pos.reshape(t_all // tt_cmb, tt_cmb, TOP_K).transpose(0, 2, 1).reshape(
            t_all // tt_cmb, TOP_K * tt_cmb)
        outs = []
        tile_off = 0
        for tr, x1 in zip(trunks, x1s):
            outs.append(_combine(pos_tbl, ys, x1, gate, tile_off=tile_off, tt=tt_cmb))
            tile_off += x1.shape[0] // tt_cmb
        for tr, o in zip(trunks, outs):
            tr["x"] = o

    return tuple(tr["x"].reshape(tr["batch"], tr["seq"], D) for tr in trunks)
```

```python
import functools
import math

import jax
import jax.numpy as jnp
import numpy as np
from jax import lax
from jax.experimental import pallas as pl
from jax.experimental.pallas import tpu as pltpu

EPS = 1e-6
ROPE_THETA = 500000.0
POOL_WINDOWS = (2, 4, 8, 16)
TOP_K = 2
POOL_HALO = 16
LANES = 128
VMEM_CAP_BYTES = 60000 * 1024
F32 = jnp.float32
BF16 = jnp.bfloat16


def _lambda_init(layer_idx):
    return 0.8 - 0.6 * math.exp(-0.3 * layer_idx)


def _vmem_limit(est_bytes):
    return int(min(VMEM_CAP_BYTES, est_bytes * 5 // 4 + (4 << 20)))


def _params(sem, est_bytes):
    return pltpu.CompilerParams(dimension_semantics=sem, vmem_limit_bytes=_vmem_limit(est_bytes))


def _inproj_kernel(x_ref, g_ref, w_ref, hg_ref, cos_ref, sx_ref, o_ref, xn_ref, *, n_qk_tiles, hd, nb):
    j = pl.program_id(1)
    tn = w_ref.shape[1]

    @pl.when(j == 0)
    def _():
        x = x_ref[...]
        ms = jnp.mean(x * x, axis=-1, keepdims=True)
        xn_ref[...] = (x * lax.rsqrt(ms + EPS) * g_ref[...]).astype(xn_ref.dtype)

    def sub_dot(sbk):
        return jnp.dot(xn_ref[...], w_ref[:, sbk * nb:(sbk + 1) * nb], preferred_element_type=F32)

    @pl.when(j < n_qk_tiles)
    def _():
        cos = cos_ref[...]
        sx = sx_ref[...]
        for sbk in range(tn // nb):
            acc = sub_dot(sbk)
            for c in range(nb // hd):
                lo = sbk * nb + c * hd
                blk = acc[:, c * hd:(c + 1) * hd]
                ms = jnp.mean(blk * blk, axis=-1, keepdims=True)
                y = blk * lax.rsqrt(ms + EPS) * hg_ref[:, lo:lo + hd]
                y = y * cos + pltpu.roll(y, hd // 2, 1) * sx
                o_ref[:, lo:lo + hd] = y.astype(o_ref.dtype)

    @pl.when(j >= n_qk_tiles)
    def _():
        for sbk in range(tn // nb):
            o_ref[:, sbk * nb:(sbk + 1) * nb] = sub_dot(sbk).astype(o_ref.dtype)


def _inproj(x2, gain, w_bf, head_gain, cos_t, sx_t, *, seq, aw, hd):
    T, D = x2.shape
    N = w_bf.shape[1]
    tm = min(512, seq)
    tn = min(1024, 2 * aw)
    nb = min(2 * hd, tn)
    n_qk_tiles = (2 * aw) // tn
    s_tiles = seq // tm
    est = 2 * tm * D * 4 + 2 * D * tn * 2 + tm * D * 2 + 2 * tm * tn * 2 + 4 * tm * nb * 4
    kern = functools.partial(_inproj_kernel, n_qk_tiles=n_qk_tiles, hd=hd, nb=nb)
    return pl.pallas_call(
        kern,
        grid=(T // tm, N // tn),
        in_specs=[
            pl.BlockSpec((tm, D), lambda i, j: (i, 0)),
            pl.BlockSpec((1, D), lambda i, j: (0, 0)),
            pl.BlockSpec((D, tn), lambda i, j: (0, j)),
            pl.BlockSpec((1, tn), lambda i, j: (0, jnp.minimum(j, n_qk_tiles - 1))),
            pl.BlockSpec((tm, hd), lambda i, j: (i % s_tiles, 0)),
            pl.BlockSpec((tm, hd), lambda i, j: (i % s_tiles, 0)),
        ],
        out_specs=pl.BlockSpec((tm, tn), lambda i, j: (i, j)),
        out_shape=jax.ShapeDtypeStruct((T, N), BF16),
        scratch_shapes=[pltpu.VMEM((tm, D), BF16)],
        compiler_params=_params(("arbitrary", "arbitrary"), est),
        name="inproj",
    )(x2, gain, w_bf, head_gain, cos_t, sx_t)


def _attn_kernel(lam_ref, subln_ref, q_ref, k_ref, v_ref, o_ref,
                 vt_ref, e_ref, mc_ref, lc_ref, acc_ref, *, tk, sub, hd, lam_init, unroll):
    seq = k_ref.shape[0]
    nck = seq // tk
    contract_last = (((1,), (1,)), ((), ()))

    @pl.when(pl.program_id(2) == 0)
    def _():
        def transpose_v(c, carry):
            vc = v_ref[pl.ds(pl.multiple_of(c * tk, tk), tk), :]
            vt_ref[c] = vc.astype(F32).T.astype(vt_ref.dtype)
            return carry

        lax.fori_loop(0, nck, transpose_v, 0)

    q = q_ref[...]
    tq = q.shape[0]
    nsub = tk // sub
    sl = 8

    def pass1(c, carry):
        kc = k_ref[pl.ds(pl.multiple_of(c * tk, tk), tk), :]
        for comp in range(2):
            st = lax.dot_general(kc[:, comp * hd:(comp + 1) * hd], q[:, comp * hd:(comp + 1) * hd],
                                 contract_last, preferred_element_type=F32)
            for j in range(nsub):
                sj = st[j * sub:(j + 1) * sub].reshape(sub // sl, sl, tq)
                mj = jnp.max(sj, axis=0)
                ej = jnp.exp2(sj - mj[None])
                mc_ref[comp, c, j] = mj
                lc_ref[comp, c, j] = jnp.sum(ej, axis=0)
                e_ref[comp, c, j * sub:(j + 1) * sub, :] = ej.reshape(sub, tq)
        return carry

    lax.fori_loop(0, nck, pass1, 0, unroll=unroll)

    lv = lam_ref[...]
    lam = (jnp.exp(jnp.sum(lv[0:1] * lv[1:2], axis=1, keepdims=True))
           - jnp.exp(jnp.sum(lv[2:3] * lv[3:4], axis=1, keepdims=True)) + lam_init)
    def over_blocks(red, x):
        return red(red(red(x, axis=0), axis=0), axis=0, keepdims=True)

    m0 = over_blocks(jnp.max, mc_ref[0])
    m1 = over_blocks(jnp.max, mc_ref[1])
    r0 = 1.0 / over_blocks(jnp.sum, lc_ref[0] * jnp.exp2(mc_ref[0] - m0))
    r1 = lam / over_blocks(jnp.sum, lc_ref[1] * jnp.exp2(mc_ref[1] - m1))
    acc_ref[...] = jnp.zeros(acc_ref.shape, F32)

    def pass2(c, carry):
        f0 = jnp.exp2(mc_ref[0, c] - m0) * r0
        f1 = jnp.exp2(mc_ref[1, c] - m1) * r1
        blocks = []
        for j in range(nsub):
            e0 = e_ref[0, c, j * sub:(j + 1) * sub, :].reshape(sub // sl, sl, tq)
            e1 = e_ref[1, c, j * sub:(j + 1) * sub, :].reshape(sub // sl, sl, tq)
            blocks.append((e0 * f0[j][None] - e1 * f1[j][None]).reshape(sub, tq))
        wt = jnp.concatenate(blocks, axis=0)
        acc_ref[...] += jnp.dot(vt_ref[c], wt.astype(vt_ref.dtype), preferred_element_type=F32)
        return carry

    lax.fori_loop(0, nck, pass2, 0, unroll=unroll)

    o = acc_ref[...].T
    ms = jnp.mean(o * o, axis=-1, keepdims=True)
    o = o * lax.rsqrt(ms + EPS) * subln_ref[...]
    o_ref[...] = (o * (1.0 - lam_init)).astype(o_ref.dtype)


def _attention(proj, lamv, subln, *, batch, seq, aw, hd, lam_init):
    T = proj.shape[0]
    vd = 2 * hd
    n_heads = aw // vd
    tq = min(seq, 512 if seq <= 2048 else 256)
    tk = min(seq, 512)
    nck = seq // tk
    q_tiles = seq // tq
    k_col0 = aw // vd
    v_col0 = 2 * aw // vd
    est = (2 * 2 * seq * vd * 2 + seq * vd * 2 + 2 * seq * tq * 4 + 2 * (nck + 2) * 8 * tq * 4
           + tq * vd * 4 + 4 * tq * vd * 2 + 6 * tq * tk * 4)
    sub = min(tk, max(8, (32 * 8 * LANES) // tq))
    nsub = tk // sub
    kern = functools.partial(_attn_kernel, tk=tk, sub=sub, hd=hd, lam_init=lam_init, unroll=min(8, nck))
    return pl.pallas_call(
        kern,
        grid=(batch, n_heads, q_tiles),
        in_specs=[
            pl.BlockSpec((4, hd), lambda b, h, i: (0, 0)),
            pl.BlockSpec((1, vd), lambda b, h, i: (0, 0)),
            pl.BlockSpec((tq, vd), lambda b, h, i: (b * q_tiles + i, h)),
            pl.BlockSpec((seq, vd), lambda b, h, i: (b, k_col0 + h)),
            pl.BlockSpec((seq, vd), lambda b, h, i: (b, v_col0 + h)),
        ],
        out_specs=pl.BlockSpec((tq, vd), lambda b, h, i: (b * q_tiles + i, h)),
        out_shape=jax.ShapeDtypeStruct((T, aw), BF16),
        scratch_shapes=[
            pltpu.VMEM((nck, vd, tk), BF16),
            pltpu.VMEM((2, nck, tk, tq), F32),
            pltpu.VMEM((2, nck, nsub, 8, tq), F32),
            pltpu.VMEM((2, nck, nsub, 8, tq), F32),
            pltpu.VMEM((vd, tq), F32),
        ],
        compiler_params=_params(("arbitrary", "arbitrary", "arbitrary"), est),
        name="diff_attention",
    )(lamv, subln, proj, proj, proj)


def _pool_bands(rows):
    t = np.arange(rows)[:, None]
    jm = np.arange(rows)[None, :]
    jh = np.arange(POOL_HALO)[None, :]
    bm, bp, bn = [], [], []
    for w in POOL_WINDOWS:
        left = w // 2
        right = w - 1 - left
        bm.append((jm >= t - left) & (jm <= t + right))
        bp.append(jh - POOL_HALO >= t - left)
        bn.append(rows + jh <= t + right)
    as_bf = lambda m: jnp.asarray(np.stack(m).astype(np.float32), dtype=BF16)
    return as_bf(bm), as_bf(bp), as_bf(bn)


def _pool_kernel(u_ref, wp_ref, ps_ref, bm_ref, bp_ref, bn_ref, o_ref, *, rows):
    g = pl.program_id(1)
    seq = u_ref.shape[0]
    n_chunks = seq // rows
    for gi, w in enumerate(POOL_WINDOWS):
        left = w // 2
        right = w - 1 - left

        @pl.when(g == gi)
        def _(gi=gi, left=left, right=right):
            def body(r, carry):
                r0 = pl.multiple_of(r * rows, rows)
                main = u_ref[pl.ds(r0, rows), :]
                p0 = pl.multiple_of(jnp.maximum(r0 - POOL_HALO, 0), POOL_HALO)
                n0 = pl.multiple_of(jnp.minimum(r0 + rows, seq - POOL_HALO), POOL_HALO)
                prev = u_ref[pl.ds(p0, POOL_HALO), :]
                nxt = u_ref[pl.ds(n0, POOL_HALO), :]
                prev = jnp.where(r > 0, prev, jnp.zeros_like(prev))
                nxt = jnp.where(r < n_chunks - 1, nxt, jnp.zeros_like(nxt))
                ssum = (jnp.dot(bm_ref[gi], main, preferred_element_type=F32)
                        + jnp.dot(bp_ref[gi], prev, preferred_element_type=F32)
                        + jnp.dot(bn_ref[gi], nxt, preferred_element_type=F32))
                t = r0 + lax.broadcasted_iota(jnp.int32, (rows, 1), 0)
                lo = jnp.maximum(t - left, 0)
                hi = jnp.minimum(t + right, seq - 1)
                cnt = (hi - lo + 1).astype(F32)
                delta = ssum / cnt - main.astype(F32)
                y = jnp.dot(delta.astype(wp_ref.dtype), wp_ref[...], preferred_element_type=F32)
                o_ref[pl.ds(r0, rows), :] = (y * ps_ref[...]).astype(o_ref.dtype)
                return carry

            lax.fori_loop(0, n_chunks, body, 0)


def _pool(proj, w_pool_bf, pool_scale, bands, *, batch, seq, aw, pw):
    T = proj.shape[0]
    n_groups, gw, _ = w_pool_bf.shape
    rows = min(seq, 512)
    u_col0 = 3 * aw // gw
    bm, bp, bn = bands
    est = (2 * 2 * seq * gw * 2 + 2 * gw * gw * 2 + 2 * 4 * rows * (rows + 2 * POOL_HALO) * 2
           + 8 * rows * gw * 4)
    kern = functools.partial(_pool_kernel, rows=rows)
    return pl.pallas_call(
        kern,
        grid=(batch, n_groups),
        in_specs=[
            pl.BlockSpec((seq, gw), lambda b, g: (b, u_col0 + g)),
            pl.BlockSpec((None, gw, gw), lambda b, g: (g, 0, 0)),
            pl.BlockSpec((1, gw), lambda b, g: (0, g)),
            pl.BlockSpec(bm.shape, lambda b, g: (0, 0, 0)),
            pl.BlockSpec(bp.shape, lambda b, g: (0, 0, 0)),
            pl.BlockSpec(bn.shape, lambda b, g: (0, 0, 0)),
        ],
        out_specs=pl.BlockSpec((seq, gw), lambda b, g: (b, g)),
        out_shape=jax.ShapeDtypeStruct((T, pw), BF16),
        compiler_params=_params(("arbitrary", "arbitrary"), est),
        name="multiscale_pool",
    )(proj, w_pool_bf, pool_scale, bm, bp, bn)


def _outproj_kernel(a_ref, p_ref, wa_ref, wp_ref, x_ref, o_ref):
    acc = (jnp.dot(a_ref[...], wa_ref[...], preferred_element_type=F32)
           + jnp.dot(p_ref[...], wp_ref[...], preferred_element_type=F32))
    o_ref[...] = x_ref[...] + acc


def _outproj(a, p, wa_bf, wp_bf, x2):
    T, D = x2.shape
    aw = a.shape[1]
    pw = p.shape[1]
    tm = min(512, T)
    tn = min(1024, D)
    est = 2 * (tm * aw * 2 + tm * pw * 2 + (aw + pw) * tn * 2 + 2 * tm * tn * 4) + 2 * tm * tn * 4
    return pl.pallas_call(
        _outproj_kernel,
        grid=(T // tm, D // tn),
        in_specs=[
            pl.BlockSpec((tm, aw), lambda i, j: (i, 0)),
            pl.BlockSpec((tm, pw), lambda i, j: (i, 0)),
            pl.BlockSpec((aw, tn), lambda i, j: (0, j)),
            pl.BlockSpec((pw, tn), lambda i, j: (0, j)),
            pl.BlockSpec((tm, tn), lambda i, j: (i, j)),
        ],
        out_specs=pl.BlockSpec((tm, tn), lambda i, j: (i, j)),
        out_shape=jax.ShapeDtypeStruct((T, D), F32),
        compiler_params=_params(("arbitrary", "arbitrary"), est),
        name="outproj",
    )(a, p, wa_bf, wp_bf, x2)


def _router_kernel(xa_ref, xb_ref, g_ref, whi_ref, wlo_ref, b_ref, xn_ref, eid_ref, gate_ref, *,
                   a_tiles, n_groups, n_experts):
    i = pl.program_id(0)
    epg_shift = int(math.log2(n_experts // n_groups))

    def run(x_ref):
        x = x_ref[...]
        ms = jnp.mean(x * x, axis=-1, keepdims=True)
        xn = x * lax.rsqrt(ms + EPS) * g_ref[...]
        hi = xn.astype(BF16)
        xn_ref[...] = hi.astype(F32)
        lo = (xn - hi.astype(F32)).astype(BF16)
        whi = whi_ref[...]
        logits = (jnp.dot(hi, whi, preferred_element_type=F32)
                  + jnp.dot(lo, whi, preferred_element_type=F32)
                  + jnp.dot(hi, wlo_ref[...], preferred_element_type=F32)
                  + b_ref[...])
        lane = lax.broadcasted_iota(jnp.int32, logits.shape, 1)
        neg = jnp.float32(-jnp.inf)
        is_g = lane < n_groups
        gl = jnp.where(is_g, logits, neg)
        gmax = jnp.max(gl, axis=1, keepdims=True)
        g_idx = jnp.min(jnp.where(gl == gmax, lane, LANES), axis=1, keepdims=True)
        gsum = jnp.sum(jnp.where(is_g, jnp.exp(gl - gmax), 0.0), axis=1, keepdims=True)
        g_w = 1.0 / gsum
        e_lane = lane - n_groups
        grp = jnp.where(jnp.logical_and(e_lane >= 0, e_lane < n_experts),
                        lax.shift_right_arithmetic(e_lane, epg_shift), -1)
        sel = grp == g_idx
        el = jnp.where(sel, logits, neg)
        emax = jnp.max(el, axis=1, keepdims=True)
        ex = jnp.where(sel, jnp.exp(el - emax), 0.0)
        prob = ex / jnp.sum(ex, axis=1, keepdims=True)
        pm = jnp.where(sel, prob, -1.0)
        p1 = jnp.max(pm, axis=1, keepdims=True)
        i1 = jnp.min(jnp.where(pm == p1, lane, LANES), axis=1, keepdims=True)
        pm2 = jnp.where(lane == i1, -1.0, pm)
        p2 = jnp.max(pm2, axis=1, keepdims=True)
        i2 = jnp.min(jnp.where(pm2 == p2, lane, LANES), axis=1, keepdims=True)
        tsum = p1 + p2
        gate1 = g_w * (p1 / tsum)
        gate2 = g_w * (p2 / tsum)
        eid_ref[...] = jnp.where(lane == 0, i1 - n_groups, jnp.where(lane == 1, i2 - n_groups, 0))
        gate_ref[...] = jnp.where(lane == 0, gate1, jnp.where(lane == 1, gate2, 0.0))

    @pl.when(i < a_tiles)
    def _():
        run(xa_ref)

    @pl.when(i >= a_tiles)
    def _():
        run(xb_ref)


def _router(x1a, x1b, gain, whi, wlo, bias, *, n_groups, n_experts):
    Ta, D = x1a.shape
    Tb = x1b.shape[0]
    tt = min(256, Ta, Tb)
    a_tiles = Ta // tt
    b_tiles = Tb // tt
    T = Ta + Tb
    est = 2 * 2 * tt * D * 4 + 2 * tt * D * 4 + 4 * D * LANES * 2 + 6 * tt * D * 4
    kern = functools.partial(_router_kernel, a_tiles=a_tiles, n_groups=n_groups, n_experts=n_experts)
    return pl.pallas_call(
        kern,
        grid=(a_tiles + b_tiles,),
        in_specs=[
            pl.BlockSpec((tt, D), lambda i: (jnp.minimum(i, a_tiles - 1), 0)),
            pl.BlockSpec((tt, D), lambda i: (jnp.maximum(i - a_tiles, 0), 0)),
            pl.BlockSpec((1, D), lambda i: (0, 0)),
            pl.BlockSpec((D, LANES), lambda i: (0, 0)),
            pl.BlockSpec((D, LANES), lambda i: (0, 0)),
            pl.BlockSpec((1, LANES), lambda i: (0, 0)),
        ],
        out_specs=[
            pl.BlockSpec((tt, D), lambda i: (i, 0)),
            pl.BlockSpec((tt, LANES), lambda i: (i, 0)),
            pl.BlockSpec((tt, LANES), lambda i: (i, 0)),
        ],
        out_shape=[
            jax.ShapeDtypeStruct((T, D), F32),
            jax.ShapeDtypeStruct((T, LANES), jnp.int32),
            jax.ShapeDtypeStruct((T, LANES), F32),
        ],
        compiler_params=_params(("arbitrary",), est),
        name="router",
    )(x1a, x1b, gain, whi, wlo, bias)


def _moe_plan(eid2, *, tm, n_experts):
    T = eid2.shape[0]
    A = T * TOP_K
    eflat = eid2.reshape(A)
    experts = jnp.arange(n_experts, dtype=jnp.int32)
    sorted_e, order = lax.sort_key_val(eflat, jnp.arange(A, dtype=jnp.int32))
    counts = jnp.sum((eflat[None, :] == experts[:, None]).astype(jnp.int32), axis=1)
    ends = jnp.cumsum(counts).astype(jnp.int32)
    starts = ends - counts
    ntile_e = (counts + tm - 1) // tm
    tile_end = jnp.cumsum(ntile_e).astype(jnp.int32)
    tile_base = tile_end - ntile_e
    n_tiles = tile_end[-1]
    G = -(-A // tm) + n_experts
    ti = jnp.arange(G, dtype=jnp.int32)
    valid = ti < n_tiles
    te = jnp.sum((ti[:, None] >= tile_end[None, :]).astype(jnp.int32), axis=1)
    te = jnp.minimum(te, n_experts - 1)
    te = jnp.where(valid, te, te[n_tiles - 1])
    j = ti - tile_base[te]
    tstart = starts[te] + j * tm
    tlen = jnp.where(valid, jnp.clip(counts[te] - j * tm, 0, tm), 0).astype(jnp.int32)
    r = jnp.arange(tm, dtype=jnp.int32)
    sidx = jnp.clip(tstart[:, None] + r[None, :], 0, A - 1)
    tok_tbl = jnp.where(r[None, :] < tlen[:, None], order[sidx] // TOP_K, 0).astype(jnp.int32)
    rank = jnp.arange(A, dtype=jnp.int32) - starts[sorted_e]
    row = tile_base[sorted_e] * tm + rank
    _, pos = lax.sort_key_val(order, row)
    pos = pos.reshape(T, TOP_K)
    return te, tlen, n_tiles.reshape(1), tok_tbl, pos, G


def _expert_kernel(te_ref, nt_ref, tl_ref, tbl_ref, xn_ref, wg_ref, wu_ref, wd_ref, o_ref,
                   idx_ref, xrow_ref, xb_ref, h_ref, sem_idx, sem_rows, *,
                   n_tiles, fa, tf, half, grp):
    del te_ref, nt_ref
    i = pl.program_id(0)
    s = pl.program_id(1)
    tm = xrow_ref.shape[0]
    n_grp = tm // grp
    ln = tl_ref[i]

    def idx_copy(tile):
        dst = idx_ref.at[pl.ds(pl.multiple_of((tile % 2) * tm, tm), tm)]
        return pltpu.make_async_copy(tbl_ref.at[tile], dst, sem_idx.at[tile % 2])

    def issue_rows(tile):
        cnt = tl_ref[tile]
        base = (tile % 2) * tm
        for g in range(n_grp):
            @pl.when(g * grp < cnt)
            def _(g=g):
                for r in range(g * grp, (g + 1) * grp):
                    tok = idx_ref[base + r]
                    pltpu.make_async_copy(xn_ref.at[tok], xrow_ref.at[r], sem_rows).start()

    def wait_rows():
        for g in range(n_grp):
            @pl.when(g * grp < ln)
            def _(g=g):
                blk = xrow_ref.at[pl.ds(g * grp, grp)]
                pltpu.make_async_copy(blk, blk, sem_rows).wait()

    @pl.when(jnp.logical_and(i == 0, s == 0))
    def _():
        xrow_ref[...] = jnp.zeros(xrow_ref.shape, xrow_ref.dtype)
        first = idx_copy(0)
        first.start()
        first.wait()
        issue_rows(0)
        if n_tiles > 1:
            idx_copy(1).start()

    @pl.when(s == 0)
    def _():
        wait_rows()
        xb_ref[...] = xrow_ref[...].astype(xb_ref.dtype)

    @pl.when(jnp.logical_and(s == 1, i + 1 < n_tiles))
    def _():
        idx_copy(i + 1).wait()
        issue_rows(i + 1)

        @pl.when(i + 2 < n_tiles)
        def _():
            idx_copy(i + 2).start()

    def rows_variants(fn):
        @pl.when(ln > half)
        def _():
            fn(tm)

        @pl.when(jnp.logical_and(ln > 0, ln <= half))
        def _():
            fn(half)

    for f in range(fa):
        @pl.when(s == f)
        def _(f=f):
            def gate_up(nrows):
                x = xb_ref[0:nrows, :]
                gate = jnp.dot(x, wg_ref[...], preferred_element_type=F32)
                up = jnp.dot(x, wu_ref[...], preferred_element_type=F32)
                h_ref[0:nrows, f * tf:(f + 1) * tf] = (jax.nn.silu(gate) * up).astype(h_ref.dtype)

            rows_variants(gate_up)

    @pl.when(s >= fa)
    def _():
        def down(nrows):
            o_ref[0:nrows, :] = jnp.dot(h_ref[0:nrows, :], wd_ref[...], preferred_element_type=F32)

        rows_variants(down)

        @pl.when(ln <= half)
        def _():
            o_ref[half:tm, :] = jnp.zeros((tm - half, o_ref.shape[1]), o_ref.dtype)

        @pl.when(ln == 0)
        def _():
            o_ref[0:half, :] = jnp.zeros((half, o_ref.shape[1]), o_ref.dtype)


def _experts(te, n_tiles, tlen, tok_tbl, xn, wg_bf, wu_bf, wd_bf, *, tm):
    G = te.shape[0]
    E, D, dff = wg_bf.shape
    tf = min(512, dff)
    fa = dff // tf
    tn = min(2048, D)
    fb = D // tn
    half = tm // 2
    grp = min(64, tm)
    assert fa >= 2, "the next tile's gather is issued at inner step 1"
    est = (tm * D * 4 + tm * D * 2 + tm * dff * 2 + 2 * 2 * D * tf * 2 + 2 * dff * tn * 2
           + 2 * tm * tn * 4 + 3 * tm * tf * 4 + tm * tn * 4)

    def a_ix(i, s, nt):
        return jnp.where(i < nt[0], jnp.minimum(s, fa - 1), fa - 1)

    def b_ix(s):
        return jnp.clip(s - fa, 0, fb - 1)

    def wd_ix(i, s, nt):
        return jnp.where(i < nt[0], b_ix(s), fb - 1)

    kern = functools.partial(_expert_kernel, n_tiles=G, fa=fa, tf=tf, half=half, grp=grp)
    return pl.pallas_call(
        kern,
        grid_spec=pltpu.PrefetchScalarGridSpec(
            num_scalar_prefetch=3,
            grid=(G, fa + fb),
            in_specs=[
                pl.BlockSpec(memory_space=pl.ANY),
                pl.BlockSpec(memory_space=pl.ANY),
                pl.BlockSpec((None, D, tf), lambda i, s, te, nt, tl: (te[i], 0, a_ix(i, s, nt))),
                pl.BlockSpec((None, D, tf), lambda i, s, te, nt, tl: (te[i], 0, a_ix(i, s, nt))),
                pl.BlockSpec((None, dff, tn), lambda i, s, te, nt, tl: (te[i], 0, wd_ix(i, s, nt))),
            ],
            out_specs=pl.BlockSpec((tm, tn), lambda i, s, te, nt, tl: (i, b_ix(s))),
            scratch_shapes=[
                pltpu.SMEM((2 * tm,), jnp.int32),
                pltpu.VMEM((tm, D), F32),
                pltpu.VMEM((tm, D), BF16),
                pltpu.VMEM((tm, dff), BF16),
                pltpu.SemaphoreType.DMA((2,)),
                pltpu.SemaphoreType.DMA(()),
            ],
        ),
        out_shape=jax.ShapeDtypeStruct((G * tm, D), F32),
        compiler_params=_params(("arbitrary", "arbitrary"), est),
        name="moe_experts",
    )(te, n_tiles, tlen, tok_tbl, xn, wg_bf, wu_bf, wd_bf)


def _combine_kernel(pos_ref, ys_ref, x1_ref, gate_ref, o_ref, idx_ref, ybuf_ref, sem_idx, sem_rows, *,
                    tile_off, n_tiles, tt):
    i = pl.program_id(0)
    slot = i % 2

    def idx_copy(tile, s):
        return pltpu.make_async_copy(pos_ref.at[tile + tile_off], idx_ref.at[s], sem_idx.at[s])

    def issue_rows(s):
        def body(r, carry):
            row = idx_ref[s, r]
            pltpu.make_async_copy(ys_ref.at[row], ybuf_ref.at[s, r], sem_rows.at[s]).start()
            return carry

        lax.fori_loop(0, TOP_K * tt, body, 0, unroll=8)

    @pl.when(i == 0)
    def _():
        first = idx_copy(0, 0)
        first.start()
        first.wait()
        issue_rows(0)
        if n_tiles > 1:
            idx_copy(1, 1).start()

    @pl.when(i + 1 < n_tiles)
    def _():
        idx_copy(i + 1, 1 - slot).wait()
        issue_rows(1 - slot)

        @pl.when(i + 2 < n_tiles)
        def _():
            idx_copy(i + 2, slot).start()

    buf = ybuf_ref.at[slot]
    pltpu.make_async_copy(buf, buf, sem_rows.at[slot]).wait()
    g = gate_ref[...]
    y0 = ybuf_ref[slot, 0:tt, :]
    y1 = ybuf_ref[slot, tt:2 * tt, :]
    o_ref[...] = x1_ref[...] + (y0 * g[:, 0:1] + y1 * g[:, 1:2])


def _combine(pos_tbl, ys, x1, gate, *, tile_off, tt):
    T, D = x1.shape
    n_tiles = T // tt
    est = 2 * TOP_K * tt * D * 4 + 2 * 2 * tt * D * 4 + 2 * tt * LANES * 4 + 3 * tt * D * 4
    kern = functools.partial(_combine_kernel, tile_off=tile_off, n_tiles=n_tiles, tt=tt)
    return pl.pallas_call(
        kern,
        grid=(n_tiles,),
        in_specs=[
            pl.BlockSpec(memory_space=pl.ANY),
            pl.BlockSpec(memory_space=pl.ANY),
            pl.BlockSpec((tt, D), lambda i: (i, 0)),
            pl.BlockSpec((tt, LANES), lambda i: (i + tile_off, 0)),
        ],
        out_specs=pl.BlockSpec((tt, D), lambda i: (i, 0)),
        out_shape=jax.ShapeDtypeStruct((T, D), F32),
        scratch_shapes=[
            pltpu.SMEM((2, TOP_K * tt), jnp.int32),
            pltpu.VMEM((2, TOP_K * tt, D), F32),
            pltpu.SemaphoreType.DMA((2,)),
            pltpu.SemaphoreType.DMA((2,)),
        ],
        compiler_params=_params(("arbitrary",), est),
        name="moe_combine",
    )(pos_tbl, ys, x1, gate)


def _rope_perm(hd, rot_dim):
    half = rot_dim // 2
    mid = hd // 2
    assert rot_dim <= mid
    perm = np.arange(hd)
    perm[half:rot_dim] = np.arange(mid, mid + half)
    perm[mid:mid + half] = np.arange(half, rot_dim)
    return perm


def _rope_tables(seq, hd, rot_dim):
    half = rot_dim // 2
    mid = hd // 2
    inv_freq = ROPE_THETA ** (-jnp.arange(half, dtype=F32) * 2.0 / rot_dim)
    ang = jnp.arange(seq, dtype=F32)[:, None] * inv_freq[None, :]
    cos, sin = jnp.cos(ang), jnp.sin(ang)
    ones = lambda n: jnp.ones((seq, n), F32)
    zeros = lambda n: jnp.zeros((seq, n), F32)
    cos_t = jnp.concatenate([cos, ones(mid - half), cos, ones(hd - mid - half)], axis=1)
    sx_t = jnp.concatenate([-sin, zeros(mid - half), sin, zeros(hd - mid - half)], axis=1)
    return cos_t, sx_t


def kernel(x_prompt, x_sample, norm_mix, w_in, q_norm, k_norm, lambda_q1, lambda_k1, lambda_q2,
           lambda_k2, subln, w_pool, pool_scale, w_out, norm_ffn, w_router_group, b_router_group,
           w_router_expert, b_router_expert, w_gate, w_up, w_down):
    n_layers, D, in_cols = w_in.shape
    mix = w_out.shape[1]
    aw = (in_cols - mix) // 2
    pw = mix - aw
    hd = q_norm.shape[-1]
    rot_dim = hd // 4
    n_groups = w_router_group.shape[-1]
    n_experts = w_router_expert.shape[-1]
    assert aw % (2 * hd) == 0 and pw % len(POOL_WINDOWS) == 0
    assert n_groups + n_experts <= LANES and (n_experts // n_groups) & (n_experts // n_groups - 1) == 0

    trunks = []
    for x in (x_prompt, x_sample):
        b, s, _ = x.shape
        trunks.append(dict(batch=b, seq=s, x=x.reshape(b * s, D), bands=_pool_bands(min(s, 512))))
    cos_t, sx_t = _rope_tables(max(tr["seq"] for tr in trunks), hd, rot_dim)
    perm = _rope_perm(hd, rot_dim)
    t_a = trunks[0]["x"].shape[0]
    t_b = trunks[1]["x"].shape[0]
    tm_moe = min(512, t_a, t_b)
    tt_cmb = min(256, t_a, t_b)

    for l in range(n_layers):
        lam_init = _lambda_init(l)
        scale = hd ** -0.5 * math.log2(math.e)
        w_qk = w_in[l, :, :2 * aw].reshape(D, 2 * aw // hd, hd)[:, :, perm].reshape(D, 2 * aw)
        w_in_bf = jnp.concatenate([w_qk, w_in[l, :, 2 * aw:]], axis=1).astype(BF16)
        head_gain = jnp.concatenate([jnp.tile(q_norm[l][perm] * scale, aw // hd),
                                     jnp.tile(k_norm[l][perm], aw // hd)]).reshape(1, 2 * aw)
        lamv = jnp.stack([lambda_q1[l], lambda_k1[l], lambda_q2[l], lambda_k2[l]]).astype(F32)
        w_pool_bf = w_pool[l].astype(BF16)
        wa_bf = w_out[l, :aw].astype(BF16)
        wp_bf = w_out[l, aw:].astype(BF16)
        w_r = jnp.concatenate([w_router_group[l], w_router_expert[l]], axis=1)
        w_r = jnp.pad(w_r, ((0, 0), (0, LANES - w_r.shape[1])))
        w_r_hi = w_r.astype(BF16)
        w_r_lo = (w_r - w_r_hi.astype(F32)).astype(BF16)
        b_r = jnp.pad(jnp.concatenate([b_router_group[l], b_router_expert[l]]).astype(F32),
                      (0, LANES - n_groups - n_experts)).reshape(1, LANES)
        wg_bf = w_gate[l].astype(BF16)
        wu_bf = w_up[l].astype(BF16)
        wd_bf = w_down[l].astype(BF16)

        x1s = []
        for tr in trunks:
            proj = _inproj(tr["x"], norm_mix[l].reshape(1, D), w_in_bf, head_gain, cos_t, sx_t,
                           seq=tr["seq"], aw=aw, hd=hd)
            a = _attention(proj, lamv, subln[l].reshape(1, 2 * hd), batch=tr["batch"], seq=tr["seq"],
                           aw=aw, hd=hd, lam_init=lam_init)
            p = _pool(proj, w_pool_bf, pool_scale[l].reshape(1, pw), tr["bands"], batch=tr["batch"],
                      seq=tr["seq"], aw=aw, pw=pw)
            x1s.append(_outproj(a, p, wa_bf, wp_bf, tr["x"]))

        xn, eid, gate = _router(x1s[0], x1s[1], norm_ffn[l].reshape(1, D), w_r_hi, w_r_lo, b_r,
                                n_groups=n_groups, n_experts=n_experts)
        te, tlen, n_tiles, tok_tbl, pos, _ = _moe_plan(eid[:, :TOP_K], tm=tm_moe, n_experts=n_experts)
        ys = _experts(te, n_tiles, tlen, tok_tbl, xn, wg_bf, wu_bf, wd_bf, tm=tm_moe)
        t_all = t_a + t_b
        pos_tbl = pos.reshape(t_all // tt_cmb, tt_cmb, TOP_K).transpose(0, 2, 1).reshape(
            t_all // tt_cmb, TOP_K * tt_cmb)
        outs = []
        tile_off = 0
        for tr, x1 in zip(trunks, x1s):
            outs.append(_combine(pos_tbl, ys, x1, gate, tile_off=tile_off, tt=tt_cmb))
            tile_off += x1.shape[0] // tt_cmb
        for tr, o in zip(trunks, outs):
            tr["x"] = o

    return tuple(tr["x"].reshape(tr["batch"], tr["seq"], D) for tr in trunks)
```

```python
import functools
import math

import jax
import jax.numpy as jnp
import numpy as np
from jax import lax
from jax.experimental import pallas as pl
from jax.experimental.pallas import tpu as pltpu

EPS = 1e-6
ROPE_THETA = 500000.0
POOL_WINDOWS = (2, 4, 8, 16)
TOP_K = 2
POOL_HALO = 16
LANES = 128
VMEM_CAP_BYTES = 60000 * 1024
F32 = jnp.float32
BF16 = jnp.bfloat16


def _lambda_init(layer_idx):
    return 0.8 - 0.6 * math.exp(-0.3 * layer_idx)


def _vmem_limit(est_bytes):
    return int(min(VMEM_CAP_BYTES, est_bytes * 5 // 4 + (4 << 20)))


def _params(sem, est_bytes):
    return pltpu.CompilerParams(dimension_semantics=sem, vmem_limit_bytes=_vmem_limit(est_bytes))


def _inproj_kernel(x_ref, g_ref, w_ref, hg_ref, cos_ref, sx_ref, o_ref, xn_ref, acc0_ref, acc1_ref, *,
                   n_qk_tiles, n_col_tiles, hd):
    j = pl.program_id(1)
    tn = w_ref.shape[1]
    accs = (acc0_ref, acc1_ref)

    @pl.when(j == 0)
    def _():
        x = x_ref[...]
        ms = jnp.mean(x * x, axis=-1, keepdims=True)
        xn_ref[...] = (x * lax.rsqrt(ms + EPS) * g_ref[...]).astype(xn_ref.dtype)

    def matmul_into(acc_ref):
        acc_ref[...] = jnp.dot(xn_ref[...], w_ref[...], preferred_element_type=F32)

    def qk_epilogue(acc_ref):
        cos = cos_ref[...]
        sx = sx_ref[...]
        for c in range(tn // hd):
            blk = acc_ref[:, c * hd:(c + 1) * hd]
            ms = jnp.mean(blk * blk, axis=-1, keepdims=True)
            y = blk * lax.rsqrt(ms + EPS) * hg_ref[:, c * hd:(c + 1) * hd]
            y = y * cos + pltpu.roll(y, hd // 2, 1) * sx
            o_ref[:, c * hd:(c + 1) * hd] = y.astype(o_ref.dtype)

    def plain_epilogue(acc_ref):
        o_ref[...] = acc_ref[...].astype(o_ref.dtype)

    for parity in range(2):
        cur, prev = accs[parity], accs[1 - parity]
        is_par = (j % 2) == parity

        @pl.when(jnp.logical_and(is_par, j == 0))
        def _(cur=cur):
            matmul_into(cur)

        @pl.when(jnp.logical_and(is_par, jnp.logical_and(j >= 1, j <= n_qk_tiles)))
        def _(cur=cur, prev=prev):
            matmul_into(cur)
            qk_epilogue(prev)

        @pl.when(jnp.logical_and(is_par, jnp.logical_and(j > n_qk_tiles, j < n_col_tiles)))
        def _(cur=cur, prev=prev):
            matmul_into(cur)
            plain_epilogue(prev)

        @pl.when(jnp.logical_and(is_par, j == n_col_tiles))
        def _(prev=prev):
            plain_epilogue(prev)


def _inproj(x2, gain, w_bf, head_gain, cos_t, sx_t, *, seq, aw, hd):
    T, D = x2.shape
    N = w_bf.shape[1]
    tm = min(512, seq)
    tn = min(1024, 2 * aw)
    n_qk_tiles = (2 * aw) // tn
    n_col_tiles = N // tn
    assert n_qk_tiles < n_col_tiles
    s_tiles = seq // tm
    est = 2 * tm * D * 4 + 2 * D * tn * 2 + tm * D * 2 + 2 * tm * tn * 2 + 4 * tm * tn * 4
    kern = functools.partial(_inproj_kernel, n_qk_tiles=n_qk_tiles, n_col_tiles=n_col_tiles, hd=hd)
    return pl.pallas_call(
        kern,
        grid=(T // tm, n_col_tiles + 1),
        in_specs=[
            pl.BlockSpec((tm, D), lambda i, j: (i, 0)),
            pl.BlockSpec((1, D), lambda i, j: (0, 0)),
            pl.BlockSpec((D, tn), lambda i, j: (0, jnp.minimum(j, n_col_tiles - 1))),
            pl.BlockSpec((1, tn), lambda i, j: (0, jnp.clip(j - 1, 0, n_qk_tiles - 1))),
            pl.BlockSpec((tm, hd), lambda i, j: (i % s_tiles, 0)),
            pl.BlockSpec((tm, hd), lambda i, j: (i % s_tiles, 0)),
        ],
        out_specs=pl.BlockSpec((tm, tn), lambda i, j: (i, jnp.maximum(j - 1, 0))),
        out_shape=jax.ShapeDtypeStruct((T, N), BF16),
        scratch_shapes=[pltpu.VMEM((tm, D), BF16), pltpu.VMEM((tm, tn), F32), pltpu.VMEM((tm, tn), F32)],
        compiler_params=_params(("arbitrary", "arbitrary"), est),
        name="inproj",
    )(x2, gain, w_bf, head_gain, cos_t, sx_t)


def _attn_kernel(lam_ref, subln_ref, q_ref, k_ref, v_ref, o_ref,
                 vt_ref, e_ref, mc_ref, lc_ref, acc_ref, *, tk, sub, hd, lam_init, unroll):
    seq = k_ref.shape[0]
    nck = seq // tk
    contract_last = (((1,), (1,)), ((), ()))

    @pl.when(pl.program_id(2) == 0)
    def _():
        def transpose_v(c, carry):
            vc = v_ref[pl.ds(pl.multiple_of(c * tk, tk), tk), :]
            vt_ref[c] = vc.astype(F32).T.astype(vt_ref.dtype)
            return carry

        lax.fori_loop(0, nck, transpose_v, 0)

    q = q_ref[...]
    tq = q.shape[0]
    nsub = tk // sub
    sl = 8

    def pass1(c, carry):
        kc = k_ref[pl.ds(pl.multiple_of(c * tk, tk), tk), :]
        for comp in range(2):
            st = lax.dot_general(kc[:, comp * hd:(comp + 1) * hd], q[:, comp * hd:(comp + 1) * hd],
                                 contract_last, preferred_element_type=F32)
            for j in range(nsub):
                sj = st[j * sub:(j + 1) * sub].reshape(sub // sl, sl, tq)
                mj = jnp.max(sj, axis=0)
                ej = jnp.exp2(sj - mj[None])
                mc_ref[comp, c, j] = mj
                lc_ref[comp, c, j] = jnp.sum(ej, axis=0)
                e_ref[comp, c, j * sub:(j + 1) * sub, :] = ej.reshape(sub, tq).astype(e_ref.dtype)
        return carry

    lax.fori_loop(0, nck, pass1, 0, unroll=unroll)

    lv = lam_ref[...]
    lam = (jnp.exp(jnp.sum(lv[0:1] * lv[1:2], axis=1, keepdims=True))
           - jnp.exp(jnp.sum(lv[2:3] * lv[3:4], axis=1, keepdims=True)) + lam_init)
    def over_blocks(red, x):
        return red(red(red(x, axis=0), axis=0), axis=0, keepdims=True)

    m0 = over_blocks(jnp.max, mc_ref[0])
    m1 = over_blocks(jnp.max, mc_ref[1])
    r0 = 1.0 / over_blocks(jnp.sum, lc_ref[0] * jnp.exp2(mc_ref[0] - m0))
    r1 = lam / over_blocks(jnp.sum, lc_ref[1] * jnp.exp2(mc_ref[1] - m1))
    acc_ref[...] = jnp.zeros(acc_ref.shape, F32)

    def pass2(c, carry):
        f0 = jnp.exp2(mc_ref[0, c] - m0) * r0
        f1 = jnp.exp2(mc_ref[1, c] - m1) * r1
        pk = 2 * sl
        blocks = []
        for j in range(nsub):
            g0 = jnp.concatenate([f0[j], f0[j]], axis=0).astype(e_ref.dtype)
            g1 = jnp.concatenate([f1[j], f1[j]], axis=0).astype(e_ref.dtype)
            e0 = e_ref[0, c, j * sub:(j + 1) * sub, :].reshape(sub // pk, pk, tq)
            e1 = e_ref[1, c, j * sub:(j + 1) * sub, :].reshape(sub // pk, pk, tq)
            blocks.append((e0 * g0[None] - e1 * g1[None]).reshape(sub, tq))
        wt = jnp.concatenate(blocks, axis=0)
        acc_ref[...] += jnp.dot(vt_ref[c], wt, preferred_element_type=F32)
        return carry

    lax.fori_loop(0, nck, pass2, 0, unroll=unroll)

    o = acc_ref[...].T
    ms = jnp.mean(o * o, axis=-1, keepdims=True)
    o = o * lax.rsqrt(ms + EPS) * subln_ref[...]
    o_ref[...] = (o * (1.0 - lam_init)).astype(o_ref.dtype)


def _attention(proj, lamv, subln, *, batch, seq, aw, hd, lam_init):
    T = proj.shape[0]
    vd = 2 * hd
    n_heads = aw // vd
    tq = min(seq, 512)
    tk = min(seq, 512)
    nck = seq // tk
    q_tiles = seq // tq
    k_col0 = aw // vd
    v_col0 = 2 * aw // vd
    sub = min(tk, 128)
    nsub = tk // sub
    est = (2 * 2 * seq * vd * 2 + seq * vd * 2 + 2 * seq * tq * 2 + 2 * 2 * nck * nsub * 8 * tq * 4
           + tq * vd * 4 + 4 * tq * vd * 2 + 6 * tq * tk * 4)
    kern = functools.partial(_attn_kernel, tk=tk, sub=sub, hd=hd, lam_init=lam_init, unroll=min(8, nck))
    return pl.pallas_call(
        kern,
        grid=(batch, n_heads, q_tiles),
        in_specs=[
            pl.BlockSpec((4, hd), lambda b, h, i: (0, 0)),
            pl.BlockSpec((1, vd), lambda b, h, i: (0, 0)),
            pl.BlockSpec((tq, vd), lambda b, h, i: (b * q_tiles + i, h)),
            pl.BlockSpec((seq, vd), lambda b, h, i: (b, k_col0 + h)),
            pl.BlockSpec((seq, vd), lambda b, h, i: (b, v_col0 + h)),
        ],
        out_specs=pl.BlockSpec((tq, vd), lambda b, h, i: (b * q_tiles + i, h)),
        out_shape=jax.ShapeDtypeStruct((T, aw), BF16),
        scratch_shapes=[
            pltpu.VMEM((nck, vd, tk), BF16),
            pltpu.VMEM((2, nck, tk, tq), BF16),
            pltpu.VMEM((2, nck, nsub, 8, tq), F32),
            pltpu.VMEM((2, nck, nsub, 8, tq), F32),
            pltpu.VMEM((vd, tq), F32),
        ],
        compiler_params=_params(("arbitrary", "arbitrary", "arbitrary"), est),
        name="diff_attention",
    )(lamv, subln, proj, proj, proj)


def _pool_bands(rows):
    t = np.arange(rows)[:, None]
    jm = np.arange(rows)[None, :]
    jh = np.arange(POOL_HALO)[None, :]
    bm, bp, bn = [], [], []
    for w in POOL_WINDOWS:
        left = w // 2
        right = w - 1 - left
        bm.append((jm >= t - left) & (jm <= t + right))
        bp.append(jh - POOL_HALO >= t - left)
        bn.append(rows + jh <= t + right)
    as_bf = lambda m: jnp.asarray(np.stack(m).astype(np.float32), dtype=BF16)
    return as_bf(bm), as_bf(bp), as_bf(bn)


def _pool_kernel(u_ref, wp_ref, ps_ref, bm_ref, bp_ref, bn_ref, o_ref, *, rows):
    g = pl.program_id(1)
    seq = u_ref.shape[0]
    n_chunks = seq // rows
    for gi, w in enumerate(POOL_WINDOWS):
        left = w // 2
        right = w - 1 - left

        @pl.when(g == gi)
        def _(gi=gi, left=left, right=right):
            def body(r, carry):
                r0 = pl.multiple_of(r * rows, rows)
                main = u_ref[pl.ds(r0, rows), :]
                p0 = pl.multiple_of(jnp.maximum(r0 - POOL_HALO, 0), POOL_HALO)
                n0 = pl.multiple_of(jnp.minimum(r0 + rows, seq - POOL_HALO), POOL_HALO)
                prev = u_ref[pl.ds(p0, POOL_HALO), :]
                nxt = u_ref[pl.ds(n0, POOL_HALO), :]
                prev = jnp.where(r > 0, prev, jnp.zeros_like(prev))
                nxt = jnp.where(r < n_chunks - 1, nxt, jnp.zeros_like(nxt))
                ssum = (jnp.dot(bm_ref[gi], main, preferred_element_type=F32)
                        + jnp.dot(bp_ref[gi], prev, preferred_element_type=F32)
                        + jnp.dot(bn_ref[gi], nxt, preferred_element_type=F32))
                t = r0 + lax.broadcasted_iota(jnp.int32, (rows, 1), 0)
                lo = jnp.maximum(t - left, 0)
                hi = jnp.minimum(t + right, seq - 1)
                cnt = (hi - lo + 1).astype(F32)
                delta = ssum / cnt - main.astype(F32)
                y = jnp.dot(delta.astype(wp_ref.dtype), wp_ref[...], preferred_element_type=F32)
                o_ref[pl.ds(r0, rows), :] = (y * ps_ref[...]).astype(o_ref.dtype)
                return carry

            lax.fori_loop(0, n_chunks, body, 0)


def _pool(proj, w_pool_bf, pool_scale, bands, *, batch, seq, aw, pw):
    T = proj.shape[0]
    n_groups, gw, _ = w_pool_bf.shape
    rows = min(seq, 512)
    u_col0 = 3 * aw // gw
    bm, bp, bn = bands
    est = (2 * 2 * seq * gw * 2 + 2 * gw * gw * 2 + 2 * 4 * rows * (rows + 2 * POOL_HALO) * 2
           + 8 * rows * gw * 4)
    kern = functools.partial(_pool_kernel, rows=rows)
    return pl.pallas_call(
        kern,
        grid=(batch, n_groups),
        in_specs=[
            pl.BlockSpec((seq, gw), lambda b, g: (b, u_col0 + g)),
            pl.BlockSpec((None, gw, gw), lambda b, g: (g, 0, 0)),
            pl.BlockSpec((1, gw), lambda b, g: (0, g)),
            pl.BlockSpec(bm.shape, lambda b, g: (0, 0, 0)),
            pl.BlockSpec(bp.shape, lambda b, g: (0, 0, 0)),
            pl.BlockSpec(bn.shape, lambda b, g: (0, 0, 0)),
        ],
        out_specs=pl.BlockSpec((seq, gw), lambda b, g: (b, g)),
        out_shape=jax.ShapeDtypeStruct((T, pw), BF16),
        compiler_params=_params(("arbitrary", "arbitrary"), est),
        name="multiscale_pool",
    )(proj, w_pool_bf, pool_scale, bm, bp, bn)


def _outproj_kernel(a_ref, p_ref, wa_ref, wp_ref, x_ref, o_ref):
    acc = (jnp.dot(a_ref[...], wa_ref[...], preferred_element_type=F32)
           + jnp.dot(p_ref[...], wp_ref[...], preferred_element_type=F32))
    o_ref[...] = x_ref[...] + acc


def _outproj(a, p, wa_bf, wp_bf, x2):
    T, D = x2.shape
    aw = a.shape[1]
    pw = p.shape[1]
    tm = min(512, T)
    tn = min(1024, D)
    est = 2 * (tm * aw * 2 + tm * pw * 2 + (aw + pw) * tn * 2 + 2 * tm * tn * 4) + 2 * tm * tn * 4
    return pl.pallas_call(
        _outproj_kernel,
        grid=(T // tm, D // tn),
        in_specs=[
            pl.BlockSpec((tm, aw), lambda i, j: (i, 0)),
            pl.BlockSpec((tm, pw), lambda i, j: (i, 0)),
            pl.BlockSpec((aw, tn), lambda i, j: (0, j)),
            pl.BlockSpec((pw, tn), lambda i, j: (0, j)),
            pl.BlockSpec((tm, tn), lambda i, j: (i, j)),
        ],
        out_specs=pl.BlockSpec((tm, tn), lambda i, j: (i, j)),
        out_shape=jax.ShapeDtypeStruct((T, D), F32),
        compiler_params=_params(("arbitrary", "arbitrary"), est),
        name="outproj",
    )(a, p, wa_bf, wp_bf, x2)


def _router_kernel(xa_ref, xb_ref, g_ref, whi_ref, wlo_ref, b_ref, xn_ref, eid_ref, gate_ref, *,
                   a_tiles, n_groups, n_experts):
    i = pl.program_id(0)
    epg_shift = int(math.log2(n_experts // n_groups))

    def run(x_ref):
        x = x_ref[...]
        ms = jnp.mean(x * x, axis=-1, keepdims=True)
        xn = x * lax.rsqrt(ms + EPS) * g_ref[...]
        hi = xn.astype(BF16)
        xn_ref[...] = hi.astype(F32)
        lo = (xn - hi.astype(F32)).astype(BF16)
        whi = whi_ref[...]
        logits = (jnp.dot(hi, whi, preferred_element_type=F32)
                  + jnp.dot(lo, whi, preferred_element_type=F32)
                  + jnp.dot(hi, wlo_ref[...], preferred_element_type=F32)
                  + b_ref[...])
        lane = lax.broadcasted_iota(jnp.int32, logits.shape, 1)
        neg = jnp.float32(-jnp.inf)
        is_g = lane < n_groups
        gl = jnp.where(is_g, logits, neg)
        gmax = jnp.max(gl, axis=1, keepdims=True)
        g_idx = jnp.min(jnp.where(gl == gmax, lane, LANES), axis=1, keepdims=True)
        gsum = jnp.sum(jnp.where(is_g, jnp.exp(gl - gmax), 0.0), axis=1, keepdims=True)
        g_w = 1.0 / gsum
        e_lane = lane - n_groups
        grp = jnp.where(jnp.logical_and(e_lane >= 0, e_lane < n_experts),
                        lax.shift_right_arithmetic(e_lane, epg_shift), -1)
        sel = grp == g_idx
        el = jnp.where(sel, logits, neg)
        emax = jnp.max(el, axis=1, keepdims=True)
        ex = jnp.where(sel, jnp.exp(el - emax), 0.0)
        prob = ex / jnp.sum(ex, axis=1, keepdims=True)
        pm = jnp.where(sel, prob, -1.0)
        p1 = jnp.max(pm, axis=1, keepdims=True)
        i1 = jnp.min(jnp.where(pm == p1, lane, LANES), axis=1, keepdims=True)
        pm2 = jnp.where(lane == i1, -1.0, pm)
        p2 = jnp.max(pm2, axis=1, keepdims=True)
        i2 = jnp.min(jnp.where(pm2 == p2, lane, LANES), axis=1, keepdims=True)
        tsum = p1 + p2
        gate1 = g_w * (p1 / tsum)
        gate2 = g_w * (p2 / tsum)
        eid_ref[...] = jnp.where(lane == 0, i1 - n_groups, jnp.where(lane == 1, i2 - n_groups, 0))
        gate_ref[...] = jnp.where(lane == 0, gate1, jnp.where(lane == 1, gate2, 0.0))

    @pl.when(i < a_tiles)
    def _():
        run(xa_ref)

    @pl.when(i >= a_tiles)
    def _():
        run(xb_ref)


def _router(x1a, x1b, gain, whi, wlo, bias, *, n_groups, n_experts):
    Ta, D = x1a.shape
    Tb = x1b.shape[0]
    tt = min(256, Ta, Tb)
    a_tiles = Ta // tt
    b_tiles = Tb // tt
    T = Ta + Tb
    est = 2 * 2 * tt * D * 4 + 2 * tt * D * 4 + 4 * D * LANES * 2 + 6 * tt * D * 4
    kern = functools.partial(_router_kernel, a_tiles=a_tiles, n_groups=n_groups, n_experts=n_experts)
    return pl.pallas_call(
        kern,
        grid=(a_tiles + b_tiles,),
        in_specs=[
            pl.BlockSpec((tt, D), lambda i: (jnp.minimum(i, a_tiles - 1), 0)),
            pl.BlockSpec((tt, D), lambda i: (jnp.maximum(i - a_tiles, 0), 0)),
            pl.BlockSpec((1, D), lambda i: (0, 0)),
            pl.BlockSpec((D, LANES), lambda i: (0, 0)),
            pl.BlockSpec((D, LANES), lambda i: (0, 0)),
            pl.BlockSpec((1, LANES), lambda i: (0, 0)),
        ],
        out_specs=[
            pl.BlockSpec((tt, D), lambda i: (i, 0)),
            pl.BlockSpec((tt, LANES), lambda i: (i, 0)),
            pl.BlockSpec((tt, LANES), lambda i: (i, 0)),
        ],
        out_shape=[
            jax.ShapeDtypeStruct((T, D), F32),
            jax.ShapeDtypeStruct((T, LANES), jnp.int32),
            jax.ShapeDtypeStruct((T, LANES), F32),
        ],
        compiler_params=_params(("arbitrary",), est),
        name="router",
    )(x1a, x1b, gain, whi, wlo, bias)


def _moe_plan(eid2, *, tm, n_experts):
    T = eid2.shape[0]
    A = T * TOP_K
    eflat = eid2.reshape(A)
    experts = jnp.arange(n_experts, dtype=jnp.int32)
    sorted_e, order = lax.sort_key_val(eflat, jnp.arange(A, dtype=jnp.int32))
    counts = jnp.sum((eflat[None, :] == experts[:, None]).astype(jnp.int32), axis=1)
    ends = jnp.cumsum(counts).astype(jnp.int32)
    starts = ends - counts
    ntile_e = (counts + tm - 1) // tm
    tile_end = jnp.cumsum(ntile_e).astype(jnp.int32)
    tile_base = tile_end - ntile_e
    n_tiles = tile_end[-1]
    G = -(-A // tm) + n_experts
    ti = jnp.arange(G, dtype=jnp.int32)
    valid = ti < n_tiles
    te = jnp.sum((ti[:, None] >= tile_end[None, :]).astype(jnp.int32), axis=1)
    te = jnp.minimum(te, n_experts - 1)
    te = jnp.where(valid, te, te[n_tiles - 1])
    j = ti - tile_base[te]
    tstart = starts[te] + j * tm
    tlen = jnp.where(valid, jnp.clip(counts[te] - j * tm, 0, tm), 0).astype(jnp.int32)
    r = jnp.arange(tm, dtype=jnp.int32)
    sidx = jnp.clip(tstart[:, None] + r[None, :], 0, A - 1)
    tok_tbl = jnp.where(r[None, :] < tlen[:, None], order[sidx] // TOP_K, 0).astype(jnp.int32)
    rank = jnp.arange(A, dtype=jnp.int32) - starts[sorted_e]
    row = tile_base[sorted_e] * tm + rank
    _, pos = lax.sort_key_val(order, row)
    pos = pos.reshape(T, TOP_K)
    return te, tlen, n_tiles.reshape(1), tok_tbl, pos, G


def _expert_kernel(te_ref, nt_ref, tl_ref, tbl_ref, xn_ref, wg_ref, wu_ref, wd_ref, o_ref,
                   idx_ref, xrow_ref, h_ref, sem_idx, sem_rows, *,
                   n_tiles, fa, tf, half, grp):
    del te_ref, nt_ref
    i = pl.program_id(0)
    s = pl.program_id(1)
    tm = xrow_ref.shape[0]
    n_grp = tm // grp
    ln = tl_ref[i]

    def idx_copy(tile):
        dst = idx_ref.at[pl.ds(pl.multiple_of((tile % 2) * tm, tm), tm)]
        return pltpu.make_async_copy(tbl_ref.at[tile], dst, sem_idx.at[tile % 2])

    def issue_rows(tile):
        cnt = tl_ref[tile]
        base = (tile % 2) * tm
        for g in range(n_grp):
            @pl.when(g * grp < cnt)
            def _(g=g):
                for r in range(g * grp, (g + 1) * grp):
                    tok = idx_ref[base + r]
                    pltpu.make_async_copy(xn_ref.at[tok], xrow_ref.at[r], sem_rows).start()

    def wait_rows():
        for g in range(n_grp):
            @pl.when(g * grp < ln)
            def _(g=g):
                blk = xrow_ref.at[pl.ds(g * grp, grp)]
                pltpu.make_async_copy(blk, blk, sem_rows).wait()

    @pl.when(jnp.logical_and(i == 0, s == 0))
    def _():
        xrow_ref[...] = jnp.zeros(xrow_ref.shape, xrow_ref.dtype)
        first = idx_copy(0)
        first.start()
        first.wait()
        issue_rows(0)
        if n_tiles > 1:
            idx_copy(1).start()

    @pl.when(s == 0)
    def _():
        wait_rows()

    @pl.when(jnp.logical_and(s == fa, i + 1 < n_tiles))
    def _():
        idx_copy(i + 1).wait()
        issue_rows(i + 1)

        @pl.when(i + 2 < n_tiles)
        def _():
            idx_copy(i + 2).start()

    def rows_variants(fn):
        @pl.when(ln > half)
        def _():
            fn(tm)

        @pl.when(jnp.logical_and(ln > 0, ln <= half))
        def _():
            fn(half)

    for f in range(fa):
        @pl.when(s == f)
        def _(f=f):
            def gate_up(nrows):
                x = xrow_ref[0:nrows, :].astype(wg_ref.dtype)
                gate = jnp.dot(x, wg_ref[...], preferred_element_type=F32)
                up = jnp.dot(x, wu_ref[...], preferred_element_type=F32)
                h_ref[0:nrows, f * tf:(f + 1) * tf] = (jax.nn.silu(gate) * up).astype(h_ref.dtype)

            rows_variants(gate_up)

    @pl.when(s >= fa)
    def _():
        def down(nrows):
            o_ref[0:nrows, :] = jnp.dot(h_ref[0:nrows, :], wd_ref[...], preferred_element_type=F32)

        rows_variants(down)

        @pl.when(ln <= half)
        def _():
            o_ref[half:tm, :] = jnp.zeros((tm - half, o_ref.shape[1]), o_ref.dtype)

        @pl.when(ln == 0)
        def _():
            o_ref[0:half, :] = jnp.zeros((half, o_ref.shape[1]), o_ref.dtype)


def _experts(te, n_tiles, tlen, tok_tbl, xn, wg_bf, wu_bf, wd_bf, *, tm):
    G = te.shape[0]
    E, D, dff = wg_bf.shape
    tf = min(512, dff)
    fa = dff // tf
    tn = min(2048, D)
    fb = D // tn
    half = tm // 2
    grp = min(64, tm)
    est = (tm * D * 4 + tm * D * 2 + tm * dff * 2 + 2 * 2 * D * tf * 2 + 2 * dff * tn * 2
           + 2 * tm * tn * 4 + 3 * tm * tf * 4 + tm * tn * 4)

    def a_ix(i, s, nt):
        return jnp.where(i < nt[0], jnp.minimum(s, fa - 1), fa - 1)

    def b_ix(s):
        return jnp.clip(s - fa, 0, fb - 1)

    def wd_ix(i, s, nt):
        return jnp.where(i < nt[0], b_ix(s), fb - 1)

    kern = functools.partial(_expert_kernel, n_tiles=G, fa=fa, tf=tf, half=half, grp=grp)
    return pl.pallas_call(
        kern,
        grid_spec=pltpu.PrefetchScalarGridSpec(
            num_scalar_prefetch=3,
            grid=(G, fa + fb),
            in_specs=[
                pl.BlockSpec(memory_space=pl.ANY),
                pl.BlockSpec(memory_space=pl.ANY),
                pl.BlockSpec((None, D, tf), lambda i, s, te, nt, tl: (te[i], 0, a_ix(i, s, nt))),
                pl.BlockSpec((None, D, tf), lambda i, s, te, nt, tl: (te[i], 0, a_ix(i, s, nt))),
                pl.BlockSpec((None, dff, tn), lambda i, s, te, nt, tl: (te[i], 0, wd_ix(i, s, nt))),
            ],
            out_specs=pl.BlockSpec((tm, tn), lambda i, s, te, nt, tl: (i, b_ix(s))),
            scratch_shapes=[
                pltpu.SMEM((2 * tm,), jnp.int32),
                pltpu.VMEM((tm, D), F32),
                pltpu.VMEM((tm, dff), BF16),
                pltpu.SemaphoreType.DMA((2,)),
                pltpu.SemaphoreType.DMA(()),
            ],
        ),
        out_shape=jax.ShapeDtypeStruct((G * tm, D), F32),
        compiler_params=_params(("arbitrary", "arbitrary"), est),
        name="moe_experts",
    )(te, n_tiles, tlen, tok_tbl, xn, wg_bf, wu_bf, wd_bf)


def _combine_kernel(pos_ref, ys_ref, x1_ref, gate_ref, o_ref, idx_ref, ybuf_ref, sem_idx, sem_rows, *,
                    tile_off, n_tiles, tt):
    i = pl.program_id(0)
    slot = i % 2

    def idx_copy(tile, s):
        return pltpu.make_async_copy(pos_ref.at[tile + tile_off], idx_ref.at[s], sem_idx.at[s])

    def issue_rows(s):
        def body(r, carry):
            row = idx_ref[s, r]
            pltpu.make_async_copy(ys_ref.at[row], ybuf_ref.at[s, r], sem_rows.at[s]).start()
            return carry

        lax.fori_loop(0, TOP_K * tt, body, 0, unroll=8)

    @pl.when(i == 0)
    def _():
        first = idx_copy(0, 0)
        first.start()
        first.wait()
        issue_rows(0)
        if n_tiles > 1:
            idx_copy(1, 1).start()

    @pl.when(i + 1 < n_tiles)
    def _():
        idx_copy(i + 1, 1 - slot).wait()
        issue_rows(1 - slot)

        @pl.when(i + 2 < n_tiles)
        def _():
            idx_copy(i + 2, slot).start()

    buf = ybuf_ref.at[slot]
    pltpu.make_async_copy(buf, buf, sem_rows.at[slot]).wait()
    g = gate_ref[...]
    y0 = ybuf_ref[slot, 0:tt, :]
    y1 = ybuf_ref[slot, tt:2 * tt, :]
    o_ref[...] = x1_ref[...] + (y0 * g[:, 0:1] + y1 * g[:, 1:2])


def _combine(pos_tbl, ys, x1, gate, *, tile_off, tt):
    T, D = x1.shape
    n_tiles = T // tt
    est = 2 * TOP_K * tt * D * 4 + 2 * 2 * tt * D * 4 + 2 * tt * LANES * 4 + 3 * tt * D * 4
    kern = functools.partial(_combine_kernel, tile_off=tile_off, n_tiles=n_tiles, tt=tt)
    return pl.pallas_call(
        kern,
        grid=(n_tiles,),
        in_specs=[
            pl.BlockSpec(memory_space=pl.ANY),
            pl.BlockSpec(memory_space=pl.ANY),
            pl.BlockSpec((tt, D), lambda i: (i, 0)),
            pl.BlockSpec((tt, LANES), lambda i: (i + tile_off, 0)),
        ],
        out_specs=pl.BlockSpec((tt, D), lambda i: (i, 0)),
        out_shape=jax.ShapeDtypeStruct((T, D), F32),
        scratch_shapes=[
            pltpu.SMEM((2, TOP_K * tt), jnp.int32),
            pltpu.VMEM((2, TOP_K * tt, D), F32),
            pltpu.SemaphoreType.DMA((2,)),
            pltpu.SemaphoreType.DMA((2,)),
        ],
        compiler_params=_params(("arbitrary",), est),
        name="moe_combine",
    )(pos_tbl, ys, x1, gate)


def _rope_perm(hd, rot_dim):
    half = rot_dim // 2
    mid = hd // 2
    assert rot_dim <= mid
    perm = np.arange(hd)
    perm[half:rot_dim] = np.arange(mid, mid + half)
    perm[mid:mid + half] = np.arange(half, rot_dim)
    return perm


def _rope_tables(seq, hd, rot_dim):
    half = rot_dim // 2
    mid = hd // 2
    inv_freq = ROPE_THETA ** (-jnp.arange(half, dtype=F32) * 2.0 / rot_dim)
    ang = jnp.arange(seq, dtype=F32)[:, None] * inv_freq[None, :]
    cos, sin = jnp.cos(ang), jnp.sin(ang)
    ones = lambda n: jnp.ones((seq, n), F32)
    zeros = lambda n: jnp.zeros((seq, n), F32)
    cos_t = jnp.concatenate([cos, ones(mid - half), cos, ones(hd - mid - half)], axis=1)
    sx_t = jnp.concatenate([-sin, zeros(mid - half), sin, zeros(hd - mid - half)], axis=1)
    return cos_t, sx_t


def kernel(x_prompt, x_sample, norm_mix, w_in, q_norm, k_norm, lambda_q1, lambda_k1, lambda_q2,
           lambda_k2, subln, w_pool, pool_scale, w_out, norm_ffn, w_router_group, b_router_group,
           w_router_expert, b_router_expert, w_gate, w_up, w_down):
    n_layers, D, in_cols = w_in.shape
    mix = w_out.shape[1]
    aw = (in_cols - mix) // 2
    pw = mix - aw
    hd = q_norm.shape[-1]
    rot_dim = hd // 4
    n_groups = w_router_group.shape[-1]
    n_experts = w_router_expert.shape[-1]
    assert aw % (2 * hd) == 0 and pw % len(POOL_WINDOWS) == 0
    assert n_groups + n_experts <= LANES and (n_experts // n_groups) & (n_experts // n_groups - 1) == 0

    trunks = []
    for x in (x_prompt, x_sample):
        b, s, _ = x.shape
        trunks.append(dict(batch=b, seq=s, x=x.reshape(b * s, D), bands=_pool_bands(min(s, 512))))
    cos_t, sx_t = _rope_tables(max(tr["seq"] for tr in trunks), hd, rot_dim)
    perm = _rope_perm(hd, rot_dim)
    t_a = trunks[0]["x"].shape[0]
    t_b = trunks[1]["x"].shape[0]
    tm_moe = min(512, t_a, t_b)
    tt_cmb = min(256, t_a, t_b)

    for l in range(n_layers):
        lam_init = _lambda_init(l)
        scale = hd ** -0.5 * math.log2(math.e)
        w_qk = w_in[l, :, :2 * aw].reshape(D, 2 * aw // hd, hd)[:, :, perm].reshape(D, 2 * aw)
        w_in_bf = jnp.concatenate([w_qk, w_in[l, :, 2 * aw:]], axis=1).astype(BF16)
        head_gain = jnp.concatenate([jnp.tile(q_norm[l][perm] * scale, aw // hd),
                                     jnp.tile(k_norm[l][perm], aw // hd)]).reshape(1, 2 * aw)
        lamv = jnp.stack([lambda_q1[l], lambda_k1[l], lambda_q2[l], lambda_k2[l]]).astype(F32)
        w_pool_bf = w_pool[l].astype(BF16)
        wa_bf = w_out[l, :aw].astype(BF16)
        wp_bf = w_out[l, aw:].astype(BF16)
        w_r = jnp.concatenate([w_router_group[l], w_router_expert[l]], axis=1)
        w_r = jnp.pad(w_r, ((0, 0), (0, LANES - w_r.shape[1])))
        w_r_hi = w_r.astype(BF16)
        w_r_lo = (w_r - w_r_hi.astype(F32)).astype(BF16)
        b_r = jnp.pad(jnp.concatenate([b_router_group[l], b_router_expert[l]]).astype(F32),
                      (0, LANES - n_groups - n_experts)).reshape(1, LANES)
        wg_bf = w_gate[l].astype(BF16)
        wu_bf = w_up[l].astype(BF16)
        wd_bf = w_down[l].astype(BF16)

        x1s = []
        for tr in trunks:
            proj = _inproj(tr["x"], norm_mix[l].reshape(1, D), w_in_bf, head_gain, cos_t, sx_t,
                           seq=tr["seq"], aw=aw, hd=hd)
            a = _attention(proj, lamv, subln[l].reshape(1, 2 * hd), batch=tr["batch"], seq=tr["seq"],
                           aw=aw, hd=hd, lam_init=lam_init)
            p = _pool(proj, w_pool_bf, pool_scale[l].reshape(1, pw), tr["bands"], batch=tr["batch"],
                      seq=tr["seq"], aw=aw, pw=pw)
            x1s.append(_outproj(a, p, wa_bf, wp_bf, tr["x"]))

        xn, eid, gate = _router(x1s[0], x1s[1], norm_ffn[l].reshape(1, D), w_r_hi, w_r_lo, b_r,
                                n_groups=n_groups, n_experts=n_experts)
        te, tlen, n_tiles, tok_tbl, pos, _ = _moe_plan(eid[:, :TOP_K], tm=tm_moe, n_experts=n_experts)
        ys = _experts(te, n_tiles, tlen, tok_tbl, xn, wg_bf, wu_bf, wd_bf, tm=tm_moe)
        t_all = t_a + t_b
        pos_tbl = pos.reshape(t_all // tt_cmb, tt_cmb, TOP_K).transpose(0, 2, 1).reshape(
            t_all // tt_cmb, TOP_K * tt_cmb)
        outs = []
        tile_off = 0
        for tr, x1 in zip(trunks, x1s):
            outs.append(_combine(pos_tbl, ys, x1, gate, tile_off=tile_off, tt=tt_cmb))
            tile_off += x1.shape[0] // tt_cmb
        for tr, o in zip(trunks, outs):
            tr["x"] = o

    return tuple(tr["x"].reshape(tr["batch"], tr["seq"], D) for tr in trunks)
```

```python
import functools
import math

import jax
import jax.numpy as jnp
import numpy as np
from jax import lax
from jax.experimental import pallas as pl
from jax.experimental.pallas import tpu as pltpu

EPS = 1e-6
ROPE_THETA = 500000.0
POOL_WINDOWS = (2, 4, 8, 16)
TOP_K = 2
POOL_HALO = 16
LANES = 128
VMEM_CAP_BYTES = 60000 * 1024
F32 = jnp.float32
BF16 = jnp.bfloat16


def _lambda_init(layer_idx):
    return 0.8 - 0.6 * math.exp(-0.3 * layer_idx)


def _vmem_limit(est_bytes):
    return int(min(VMEM_CAP_BYTES, est_bytes * 5 // 4 + (4 << 20)))


def _params(sem, est_bytes):
    return pltpu.CompilerParams(dimension_semantics=sem, vmem_limit_bytes=_vmem_limit(est_bytes))


def _cast_block_rows(rows, steps):
    rb = -(-rows // steps)
    while rows % rb or rb % 16:
        rb += 1
    return rb


def _inproj_kernel(x_ref, g_ref, w_ref, hg_ref, cos_ref, sx_ref, wc_ref, o_ref, wc_out_ref,
                   xn_ref, acc0_ref, acc1_ref, *, n_qk_tiles, n_col_tiles, hd):
    j = pl.program_id(1)
    tn = w_ref.shape[1]
    accs = (acc0_ref, acc1_ref)

    def cast_side():
        wc_out_ref[...] = wc_ref[...].astype(wc_out_ref.dtype)

    @pl.when(j == 0)
    def _():
        x = x_ref[...]
        ms = jnp.mean(x * x, axis=-1, keepdims=True)
        xn_ref[...] = (x * lax.rsqrt(ms + EPS) * g_ref[...]).astype(xn_ref.dtype)

    def matmul_into(acc_ref):
        acc_ref[...] = jnp.dot(xn_ref[...], w_ref[...], preferred_element_type=F32)

    def qk_epilogue(acc_ref):
        cos = cos_ref[...]
        sx = sx_ref[...]
        for c in range(tn // hd):
            blk = acc_ref[:, c * hd:(c + 1) * hd]
            ms = jnp.mean(blk * blk, axis=-1, keepdims=True)
            y = blk * lax.rsqrt(ms + EPS) * hg_ref[:, c * hd:(c + 1) * hd]
            y = y * cos + pltpu.roll(y, hd // 2, 1) * sx
            o_ref[:, c * hd:(c + 1) * hd] = y.astype(o_ref.dtype)

    def plain_epilogue(acc_ref):
        o_ref[...] = acc_ref[...].astype(o_ref.dtype)

    @pl.when(j == 0)
    def _():
        cast_side()
        matmul_into(accs[0])

    for parity in range(2):
        cur, prev = accs[parity], accs[1 - parity]
        is_par = (j % 2) == parity

        @pl.when(jnp.logical_and(is_par, jnp.logical_and(j >= 1, j <= n_qk_tiles)))
        def _(cur=cur, prev=prev):
            cast_side()
            matmul_into(cur)
            qk_epilogue(prev)

        @pl.when(jnp.logical_and(is_par, jnp.logical_and(j > n_qk_tiles, j < n_col_tiles)))
        def _(cur=cur, prev=prev):
            cast_side()
            matmul_into(cur)
            plain_epilogue(prev)

    @pl.when(j == n_col_tiles)
    def _():
        cast_side()
        plain_epilogue(accs[(n_col_tiles - 1) % 2])


def _inproj(x2, gain, w_bf, head_gain, cos_t, sx_t, side_w, *, seq, aw, hd):
    T, D = x2.shape
    N = w_bf.shape[1]
    tm = min(512, seq)
    tn = min(1024, 2 * aw)
    n_qk_tiles = (2 * aw) // tn
    n_col_tiles = N // tn
    assert n_qk_tiles < n_col_tiles
    s_tiles = seq // tm
    n_steps = n_col_tiles + 1
    rows, cols = side_w.shape
    rb = _cast_block_rows(rows, (T // tm) * n_steps)
    cast_blocks = rows // rb
    cast_ix = lambda i, j: (jnp.minimum(i * n_steps + j, cast_blocks - 1), 0)
    est = (2 * tm * D * 4 + 2 * D * tn * 2 + tm * D * 2 + 2 * tm * tn * 2 + 4 * tm * tn * 4
           + 2 * rb * cols * (4 + 2))
    kern = functools.partial(_inproj_kernel, n_qk_tiles=n_qk_tiles, n_col_tiles=n_col_tiles, hd=hd)
    return pl.pallas_call(
        kern,
        grid=(T // tm, n_steps),
        in_specs=[
            pl.BlockSpec((tm, D), lambda i, j: (i, 0)),
            pl.BlockSpec((1, D), lambda i, j: (0, 0)),
            pl.BlockSpec((D, tn), lambda i, j: (0, jnp.minimum(j, n_col_tiles - 1))),
            pl.BlockSpec((1, tn), lambda i, j: (0, jnp.clip(j - 1, 0, n_qk_tiles - 1))),
            pl.BlockSpec((tm, hd), lambda i, j: (i % s_tiles, 0)),
            pl.BlockSpec((tm, hd), lambda i, j: (i % s_tiles, 0)),
            pl.BlockSpec((rb, cols), cast_ix),
        ],
        out_specs=[
            pl.BlockSpec((tm, tn), lambda i, j: (i, jnp.maximum(j - 1, 0))),
            pl.BlockSpec((rb, cols), cast_ix),
        ],
        out_shape=[jax.ShapeDtypeStruct((T, N), BF16), jax.ShapeDtypeStruct((rows, cols), BF16)],
        scratch_shapes=[pltpu.VMEM((tm, D), BF16), pltpu.VMEM((tm, tn), F32), pltpu.VMEM((tm, tn), F32)],
        compiler_params=_params(("arbitrary", "arbitrary"), est),
        name="inproj",
    )(x2, gain, w_bf, head_gain, cos_t, sx_t, side_w)


def _attn_kernel(lam_ref, subln_ref, q_ref, k_ref, v_ref, o_ref,
                 vt_ref, e_ref, mc_ref, lc_ref, acc_ref, *, tk, sub, hd, lam_init, unroll):
    seq = k_ref.shape[0]
    nck = seq // tk
    contract_last = (((1,), (1,)), ((), ()))

    @pl.when(pl.program_id(2) == 0)
    def _():
        def transpose_v(c, carry):
            vc = v_ref[pl.ds(pl.multiple_of(c * tk, tk), tk), :]
            vt_ref[c] = vc.astype(F32).T.astype(vt_ref.dtype)
            return carry

        lax.fori_loop(0, nck, transpose_v, 0)

    q = q_ref[...]
    tq = q.shape[0]
    nsub = tk // sub
    sl = 8

    def pass1(c, carry):
        kc = k_ref[pl.ds(pl.multiple_of(c * tk, tk), tk), :]
        for comp in range(2):
            st = lax.dot_general(kc[:, comp * hd:(comp + 1) * hd], q[:, comp * hd:(comp + 1) * hd],
                                 contract_last, preferred_element_type=F32)
            for j in range(nsub):
                sj = st[j * sub:(j + 1) * sub].reshape(sub // sl, sl, tq)
                mj = jnp.max(sj, axis=0)
                ej = jnp.exp2(sj - mj[None])
                mc_ref[comp, c, j] = mj
                lc_ref[comp, c, j] = jnp.sum(ej, axis=0)
                e_ref[comp, c, j * sub:(j + 1) * sub, :] = ej.reshape(sub, tq).astype(e_ref.dtype)
        return carry

    lax.fori_loop(0, nck, pass1, 0, unroll=unroll)

    lv = lam_ref[...]
    lam = (jnp.exp(jnp.sum(lv[0:1] * lv[1:2], axis=1, keepdims=True))
           - jnp.exp(jnp.sum(lv[2:3] * lv[3:4], axis=1, keepdims=True)) + lam_init)
    def over_blocks(red, x):
        return red(red(red(x, axis=0), axis=0), axis=0, keepdims=True)

    m0 = over_blocks(jnp.max, mc_ref[0])
    m1 = over_blocks(jnp.max, mc_ref[1])
    r0 = 1.0 / over_blocks(jnp.sum, lc_ref[0] * jnp.exp2(mc_ref[0] - m0))
    r1 = lam / over_blocks(jnp.sum, lc_ref[1] * jnp.exp2(mc_ref[1] - m1))
    acc_ref[...] = jnp.zeros(acc_ref.shape, F32)

    def pass2(c, carry):
        f0 = jnp.exp2(mc_ref[0, c] - m0) * r0
        f1 = jnp.exp2(mc_ref[1, c] - m1) * r1
        pk = 2 * sl
        blocks = []
        for j in range(nsub):
            g0 = jnp.concatenate([f0[j], f0[j]], axis=0).astype(e_ref.dtype)
            g1 = jnp.concatenate([f1[j], f1[j]], axis=0).astype(e_ref.dtype)
            e0 = e_ref[0, c, j * sub:(j + 1) * sub, :].reshape(sub // pk, pk, tq)
            e1 = e_ref[1, c, j * sub:(j + 1) * sub, :].reshape(sub // pk, pk, tq)
            blocks.append((e0 * g0[None] - e1 * g1[None]).reshape(sub, tq))
        wt = jnp.concatenate(blocks, axis=0)
        acc_ref[...] += jnp.dot(vt_ref[c], wt, preferred_element_type=F32)
        return carry

    lax.fori_loop(0, nck, pass2, 0, unroll=unroll)

    o = acc_ref[...].T
    ms = jnp.mean(o * o, axis=-1, keepdims=True)
    o = o * lax.rsqrt(ms + EPS) * subln_ref[...]
    o_ref[...] = (o * (1.0 - lam_init)).astype(o_ref.dtype)


def _attention(proj, lamv, subln, *, batch, seq, aw, hd, lam_init):
    T = proj.shape[0]
    vd = 2 * hd
    n_heads = aw // vd
    tq = min(seq, 512)
    tk = min(seq, 512)
    nck = seq // tk
    q_tiles = seq // tq
    k_col0 = aw // vd
    v_col0 = 2 * aw // vd
    sub = min(tk, 128)
    nsub = tk // sub
    est = (2 * 2 * seq * vd * 2 + seq * vd * 2 + 2 * seq * tq * 2 + 2 * 2 * nck * nsub * 8 * tq * 4
           + tq * vd * 4 + 4 * tq * vd * 2 + 6 * tq * tk * 4)
    kern = functools.partial(_attn_kernel, tk=tk, sub=sub, hd=hd, lam_init=lam_init, unroll=min(8, nck))
    return pl.pallas_call(
        kern,
        grid=(batch, n_heads, q_tiles),
        in_specs=[
            pl.BlockSpec((4, hd), lambda b, h, i: (0, 0)),
            pl.BlockSpec((1, vd), lambda b, h, i: (0, 0)),
            pl.BlockSpec((tq, vd), lambda b, h, i: (b * q_tiles + i, h)),
            pl.BlockSpec((seq, vd), lambda b, h, i: (b, k_col0 + h)),
            pl.BlockSpec((seq, vd), lambda b, h, i: (b, v_col0 + h)),
        ],
        out_specs=pl.BlockSpec((tq, vd), lambda b, h, i: (b * q_tiles + i, h)),
        out_shape=jax.ShapeDtypeStruct((T, aw), BF16),
        scratch_shapes=[
            pltpu.VMEM((nck, vd, tk), BF16),
            pltpu.VMEM((2, nck, tk, tq), BF16),
            pltpu.VMEM((2, nck, nsub, 8, tq), F32),
            pltpu.VMEM((2, nck, nsub, 8, tq), F32),
            pltpu.VMEM((vd, tq), F32),
        ],
        compiler_params=_params(("arbitrary", "arbitrary", "arbitrary"), est),
        name="diff_attention",
    )(lamv, subln, proj, proj, proj)


def _pool_bands(rows):
    t = np.arange(rows)[:, None]
    jm = np.arange(rows)[None, :]
    jh = np.arange(POOL_HALO)[None, :]
    bm, bp, bn = [], [], []
    for w in POOL_WINDOWS:
        left = w // 2
        right = w - 1 - left
        bm.append((jm >= t - left) & (jm <= t + right))
        bp.append(jh - POOL_HALO >= t - left)
        bn.append(rows + jh <= t + right)
    as_bf = lambda m: jnp.asarray(np.stack(m).astype(np.float32), dtype=BF16)
    return as_bf(bm), as_bf(bp), as_bf(bn)


def _pool_kernel(u_ref, wp_ref, ps_ref, bm_ref, bp_ref, bn_ref, o_ref, *, rows):
    g = pl.program_id(1)
    seq = u_ref.shape[0]
    n_chunks = seq // rows
    for gi, w in enumerate(POOL_WINDOWS):
        left = w // 2
        right = w - 1 - left

        @pl.when(g == gi)
        def _(gi=gi, left=left, right=right):
            def body(r, carry):
                r0 = pl.multiple_of(r * rows, rows)
                main = u_ref[pl.ds(r0, rows), :]
                p0 = pl.multiple_of(jnp.maximum(r0 - POOL_HALO, 0), POOL_HALO)
                n0 = pl.multiple_of(jnp.minimum(r0 + rows, seq - POOL_HALO), POOL_HALO)
                prev = u_ref[pl.ds(p0, POOL_HALO), :]
                nxt = u_ref[pl.ds(n0, POOL_HALO), :]
                prev = jnp.where(r > 0, prev, jnp.zeros_like(prev))
                nxt = jnp.where(r < n_chunks - 1, nxt, jnp.zeros_like(nxt))
                ssum = (jnp.dot(bm_ref[gi], main, preferred_element_type=F32)
                        + jnp.dot(bp_ref[gi], prev, preferred_element_type=F32)
                        + jnp.dot(bn_ref[gi], nxt, preferred_element_type=F32))
                t = r0 + lax.broadcasted_iota(jnp.int32, (rows, 1), 0)
                lo = jnp.maximum(t - left, 0)
                hi = jnp.minimum(t + right, seq - 1)
                cnt = (hi - lo + 1).astype(F32)
                delta = ssum / cnt - main.astype(F32)
                y = jnp.dot(delta.astype(wp_ref.dtype), wp_ref[...], preferred_element_type=F32)
                o_ref[pl.ds(r0, rows), :] = (y * ps_ref[...]).astype(o_ref.dtype)
                return carry

            lax.fori_loop(0, n_chunks, body, 0)


def _pool(proj, w_pool_bf, pool_scale, bands, *, batch, seq, aw, pw):
    T = proj.shape[0]
    n_groups, gw, _ = w_pool_bf.shape
    rows = min(seq, 512)
    u_col0 = 3 * aw // gw
    bm, bp, bn = bands
    est = (2 * 2 * seq * gw * 2 + 2 * gw * gw * 2 + 2 * 4 * rows * (rows + 2 * POOL_HALO) * 2
           + 8 * rows * gw * 4)
    kern = functools.partial(_pool_kernel, rows=rows)
    return pl.pallas_call(
        kern,
        grid=(batch, n_groups),
        in_specs=[
            pl.BlockSpec((seq, gw), lambda b, g: (b, u_col0 + g)),
            pl.BlockSpec((None, gw, gw), lambda b, g: (g, 0, 0)),
            pl.BlockSpec((1, gw), lambda b, g: (0, g)),
            pl.BlockSpec(bm.shape, lambda b, g: (0, 0, 0)),
            pl.BlockSpec(bp.shape, lambda b, g: (0, 0, 0)),
            pl.BlockSpec(bn.shape, lambda b, g: (0, 0, 0)),
        ],
        out_specs=pl.BlockSpec((seq, gw), lambda b, g: (b, g)),
        out_shape=jax.ShapeDtypeStruct((T, pw), BF16),
        compiler_params=_params(("arbitrary", "arbitrary"), est),
        name="multiscale_pool",
    )(proj, w_pool_bf, pool_scale, bm, bp, bn)


def _outproj_kernel(a_ref, p_ref, wa_ref, wp_ref, x_ref, *rest):
    if len(rest) == 3:
        wc_ref, o_ref, wc_out_ref = rest
        wc_out_ref[...] = wc_ref[...].astype(wc_out_ref.dtype)
    else:
        (o_ref,) = rest
    acc = (jnp.dot(a_ref[...], wa_ref[...], preferred_element_type=F32)
           + jnp.dot(p_ref[...], wp_ref[...], preferred_element_type=F32))
    o_ref[...] = x_ref[...] + acc


def _outproj(a, p, wa_bf, wp_bf, x2, side_w=None):
    T, D = x2.shape
    aw = a.shape[1]
    pw = p.shape[1]
    tm = min(512, T)
    tn = min(1024, D)
    n_col = D // tn
    est = 2 * (tm * aw * 2 + tm * pw * 2 + (aw + pw) * tn * 2 + 2 * tm * tn * 4) + 2 * tm * tn * 4
    in_specs = [
        pl.BlockSpec((tm, aw), lambda i, j: (i, 0)),
        pl.BlockSpec((tm, pw), lambda i, j: (i, 0)),
        pl.BlockSpec((aw, tn), lambda i, j: (0, j)),
        pl.BlockSpec((pw, tn), lambda i, j: (0, j)),
        pl.BlockSpec((tm, tn), lambda i, j: (i, j)),
    ]
    out_specs = [pl.BlockSpec((tm, tn), lambda i, j: (i, j))]
    out_shape = [jax.ShapeDtypeStruct((T, D), F32)]
    args = [a, p, wa_bf, wp_bf, x2]
    if side_w is not None:
        rows, cols = side_w.shape
        rb = _cast_block_rows(rows, (T // tm) * n_col)
        cast_blocks = rows // rb
        cast_ix = lambda i, j: (jnp.minimum(i * n_col + j, cast_blocks - 1), 0)
        in_specs.append(pl.BlockSpec((rb, cols), cast_ix))
        out_specs.append(pl.BlockSpec((rb, cols), cast_ix))
        out_shape.append(jax.ShapeDtypeStruct((rows, cols), BF16))
        args.append(side_w)
        est += 2 * rb * cols * (4 + 2)
    res = pl.pallas_call(
        _outproj_kernel,
        grid=(T // tm, n_col),
        in_specs=in_specs,
        out_specs=out_specs,
        out_shape=out_shape,
        compiler_params=_params(("arbitrary", "arbitrary"), est),
        name="outproj",
    )(*args)
    return res if side_w is not None else res[0]


def _router_kernel(xa_ref, xb_ref, g_ref, whi_ref, wlo_ref, b_ref, xn_ref, eid_ref, gate_ref, *,
                   a_tiles, n_groups, n_experts):
    i = pl.program_id(0)
    epg_shift = int(math.log2(n_experts // n_groups))

    def run(x_ref):
        x = x_ref[...]
        ms = jnp.mean(x * x, axis=-1, keepdims=True)
        xn = x * lax.rsqrt(ms + EPS) * g_ref[...]
        hi = xn.astype(BF16)
        xn_ref[...] = hi.astype(F32)
        lo = (xn - hi.astype(F32)).astype(BF16)
        whi = whi_ref[...]
        logits = (jnp.dot(hi, whi, preferred_element_type=F32)
                  + jnp.dot(lo, whi, preferred_element_type=F32)
                  + jnp.dot(hi, wlo_ref[...], preferred_element_type=F32)
                  + b_ref[...])
        lane = lax.broadcasted_iota(jnp.int32, logits.shape, 1)
        neg = jnp.float32(-jnp.inf)
        is_g = lane < n_groups
        gl = jnp.where(is_g, logits, neg)
        gmax = jnp.max(gl, axis=1, keepdims=True)
        g_idx = jnp.min(jnp.where(gl == gmax, lane, LANES), axis=1, keepdims=True)
        gsum = jnp.sum(jnp.where(is_g, jnp.exp(gl - gmax), 0.0), axis=1, keepdims=True)
        g_w = 1.0 / gsum
        e_lane = lane - n_groups
        grp = jnp.where(jnp.logical_and(e_lane >= 0, e_lane < n_experts),
                        lax.shift_right_arithmetic(e_lane, epg_shift), -1)
        sel = grp == g_idx
        el = jnp.where(sel, logits, neg)
        emax = jnp.max(el, axis=1, keepdims=True)
        ex = jnp.where(sel, jnp.exp(el - emax), 0.0)
        prob = ex / jnp.sum(ex, axis=1, keepdims=True)
        pm = jnp.where(sel, prob, -1.0)
        p1 = jnp.max(pm, axis=1, keepdims=True)
        i1 = jnp.min(jnp.where(pm == p1, lane, LANES), axis=1, keepdims=True)
        pm2 = jnp.where(lane == i1, -1.0, pm)
        p2 = jnp.max(pm2, axis=1, keepdims=True)
        i2 = jnp.min(jnp.where(pm2 == p2, lane, LANES), axis=1, keepdims=True)
        tsum = p1 + p2
        gate1 = g_w * (p1 / tsum)
        gate2 = g_w * (p2 / tsum)
        eid_ref[...] = jnp.where(lane == 0, i1 - n_groups, jnp.where(lane == 1, i2 - n_groups, 0))
        gate_ref[...] = jnp.where(lane == 0, gate1, jnp.where(lane == 1, gate2, 0.0))

    @pl.when(i < a_tiles)
    def _():
        run(xa_ref)

    @pl.when(i >= a_tiles)
    def _():
        run(xb_ref)


def _router(x1a, x1b, gain, whi, wlo, bias, *, n_groups, n_experts):
    Ta, D = x1a.shape
    Tb = x1b.shape[0]
    tt = min(256, Ta, Tb)
    a_tiles = Ta // tt
    b_tiles = Tb // tt
    T = Ta + Tb
    est = 2 * 2 * tt * D * 4 + 2 * tt * D * 4 + 4 * D * LANES * 2 + 6 * tt * D * 4
    kern = functools.partial(_router_kernel, a_tiles=a_tiles, n_groups=n_groups, n_experts=n_experts)
    return pl.pallas_call(
        kern,
        grid=(a_tiles + b_tiles,),
        in_specs=[
            pl.BlockSpec((tt, D), lambda i: (jnp.minimum(i, a_tiles - 1), 0)),
            pl.BlockSpec((tt, D), lambda i: (jnp.maximum(i - a_tiles, 0), 0)),
            pl.BlockSpec((1, D), lambda i: (0, 0)),
            pl.BlockSpec((D, LANES), lambda i: (0, 0)),
            pl.BlockSpec((D, LANES), lambda i: (0, 0)),
            pl.BlockSpec((1, LANES), lambda i: (0, 0)),
        ],
        out_specs=[
            pl.BlockSpec((tt, D), lambda i: (i, 0)),
            pl.BlockSpec((tt, LANES), lambda i: (i, 0)),
            pl.BlockSpec((tt, LANES), lambda i: (i, 0)),
        ],
        out_shape=[
            jax.ShapeDtypeStruct((T, D), F32),
            jax.ShapeDtypeStruct((T, LANES), jnp.int32),
            jax.ShapeDtypeStruct((T, LANES), F32),
        ],
        compiler_params=_params(("arbitrary",), est),
        name="router",
    )(x1a, x1b, gain, whi, wlo, bias)


def _moe_plan(eid2, *, tm, n_experts):
    T = eid2.shape[0]
    A = T * TOP_K
    eflat = eid2.reshape(A)
    experts = jnp.arange(n_experts, dtype=jnp.int32)
    sorted_e, order = lax.sort_key_val(eflat, jnp.arange(A, dtype=jnp.int32))
    counts = jnp.sum((eflat[None, :] == experts[:, None]).astype(jnp.int32), axis=1)
    ends = jnp.cumsum(counts).astype(jnp.int32)
    starts = ends - counts
    ntile_e = (counts + tm - 1) // tm
    tile_end = jnp.cumsum(ntile_e).astype(jnp.int32)
    tile_base = tile_end - ntile_e
    n_tiles = tile_end[-1]
    G = -(-A // tm) + n_experts
    ti = jnp.arange(G, dtype=jnp.int32)
    valid = ti < n_tiles
    te = jnp.sum((ti[:, None] >= tile_end[None, :]).astype(jnp.int32), axis=1)
    te = jnp.minimum(te, n_experts - 1)
    te = jnp.where(valid, te, te[n_tiles - 1])
    j = ti - tile_base[te]
    tstart = starts[te] + j * tm
    tlen = jnp.where(valid, jnp.clip(counts[te] - j * tm, 0, tm), 0).astype(jnp.int32)
    r = jnp.arange(tm, dtype=jnp.int32)
    sidx = jnp.clip(tstart[:, None] + r[None, :], 0, A - 1)
    tok_tbl = jnp.where(r[None, :] < tlen[:, None], order[sidx] // TOP_K, 0).astype(jnp.int32)
    rank = jnp.arange(A, dtype=jnp.int32) - starts[sorted_e]
    row = tile_base[sorted_e] * tm + rank
    _, pos = lax.sort_key_val(order, row)
    pos = pos.reshape(T, TOP_K)
    return te, tlen, n_tiles.reshape(1), tok_tbl, pos, G


def _expert_kernel(te_ref, nt_ref, tl_ref, tbl_ref, xn_ref, wg_ref, wu_ref, wd_ref, o_ref,
                   idx_ref, xrow_ref, h_ref, sem_idx, sem_rows, *,
                   n_tiles, fa, tf, half, grp):
    del te_ref, nt_ref
    i = pl.program_id(0)
    s = pl.program_id(1)
    tm = xrow_ref.shape[0]
    n_grp = tm // grp
    ln = tl_ref[i]

    def idx_copy(tile):
        dst = idx_ref.at[pl.ds(pl.multiple_of((tile % 2) * tm, tm), tm)]
        return pltpu.make_async_copy(tbl_ref.at[tile], dst, sem_idx.at[tile % 2])

    def issue_rows(tile):
        cnt = tl_ref[tile]
        base = (tile % 2) * tm
        for g in range(n_grp):
            @pl.when(g * grp < cnt)
            def _(g=g):
                for r in range(g * grp, (g + 1) * grp):
                    tok = idx_ref[base + r]
                    pltpu.make_async_copy(xn_ref.at[tok], xrow_ref.at[r], sem_rows).start()

    def wait_rows():
        for g in range(n_grp):
            @pl.when(g * grp < ln)
            def _(g=g):
                blk = xrow_ref.at[pl.ds(g * grp, grp)]
                pltpu.make_async_copy(blk, blk, sem_rows).wait()

    @pl.when(jnp.logical_and(i == 0, s == 0))
    def _():
        xrow_ref[...] = jnp.zeros(xrow_ref.shape, xrow_ref.dtype)
        first = idx_copy(0)
        first.start()
        first.wait()
        issue_rows(0)
        if n_tiles > 1:
            idx_copy(1).start()

    @pl.when(s == 0)
    def _():
        wait_rows()

    @pl.when(jnp.logical_and(s == fa, i + 1 < n_tiles))
    def _():
        idx_copy(i + 1).wait()
        issue_rows(i + 1)

        @pl.when(i + 2 < n_tiles)
        def _():
            idx_copy(i + 2).start()

    def rows_variants(fn):
        @pl.when(ln > half)
        def _():
            fn(tm)

        @pl.when(jnp.logical_and(ln > 0, ln <= half))
        def _():
            fn(half)

    for f in range(fa):
        @pl.when(s == f)
        def _(f=f):
            def gate_up(nrows):
                x = xrow_ref[0:nrows, :].astype(wg_ref.dtype)
                gate = jnp.dot(x, wg_ref[...], preferred_element_type=F32)
                up = jnp.dot(x, wu_ref[...], preferred_element_type=F32)
                h_ref[0:nrows, f * tf:(f + 1) * tf] = (jax.nn.silu(gate) * up).astype(h_ref.dtype)

            rows_variants(gate_up)

    @pl.when(s >= fa)
    def _():
        def down(nrows):
            o_ref[0:nrows, :] = jnp.dot(h_ref[0:nrows, :], wd_ref[...], preferred_element_type=F32)

        rows_variants(down)

        @pl.when(ln <= half)
        def _():
            o_ref[half:tm, :] = jnp.zeros((tm - half, o_ref.shape[1]), o_ref.dtype)

        @pl.when(ln == 0)
        def _():
            o_ref[0:half, :] = jnp.zeros((half, o_ref.shape[1]), o_ref.dtype)


def _experts(te, n_tiles, tlen, tok_tbl, xn, wg_bf, wu_bf, wd_bf, *, tm):
    G = te.shape[0]
    E, D, dff = wg_bf.shape
    tf = min(512, dff)
    fa = dff // tf
    tn = min(2048, D)
    fb = D // tn
    half = tm // 2
    grp = min(64, tm)
    est = (tm * D * 4 + tm * D * 2 + tm * dff * 2 + 2 * 2 * D * tf * 2 + 2 * dff * tn * 2
           + 2 * tm * tn * 4 + 3 * tm * tf * 4 + tm * tn * 4)

    def a_ix(i, s, nt):
        return jnp.where(i < nt[0], jnp.minimum(s, fa - 1), fa - 1)

    def b_ix(s):
        return jnp.clip(s - fa, 0, fb - 1)

    def wd_ix(i, s, nt):
        return jnp.where(i < nt[0], b_ix(s), fb - 1)

    kern = functools.partial(_expert_kernel, n_tiles=G, fa=fa, tf=tf, half=half, grp=grp)
    return pl.pallas_call(
        kern,
        grid_spec=pltpu.PrefetchScalarGridSpec(
            num_scalar_prefetch=3,
            grid=(G, fa + fb),
            in_specs=[
                pl.BlockSpec(memory_space=pl.ANY),
                pl.BlockSpec(memory_space=pl.ANY),
                pl.BlockSpec((None, D, tf), lambda i, s, te, nt, tl: (te[i], 0, a_ix(i, s, nt))),
                pl.BlockSpec((None, D, tf), lambda i, s, te, nt, tl: (te[i], 0, a_ix(i, s, nt))),
                pl.BlockSpec((None, dff, tn), lambda i, s, te, nt, tl: (te[i], 0, wd_ix(i, s, nt))),
            ],
            out_specs=pl.BlockSpec((tm, tn), lambda i, s, te, nt, tl: (i, b_ix(s))),
            scratch_shapes=[
                pltpu.SMEM((2 * tm,), jnp.int32),
                pltpu.VMEM((tm, D), F32),
                pltpu.VMEM((tm, dff), BF16),
                pltpu.SemaphoreType.DMA((2,)),
                pltpu.SemaphoreType.DMA(()),
            ],
        ),
        out_shape=jax.ShapeDtypeStruct((G * tm, D), F32),
        compiler_params=_params(("arbitrary", "arbitrary"), est),
        name="moe_experts",
    )(te, n_tiles, tlen, tok_tbl, xn, wg_bf, wu_bf, wd_bf)


def _combine_kernel(pos_ref, ys_ref, x1_ref, gate_ref, o_ref, idx_ref, ybuf_ref, sem_idx, sem_rows, *,
                    tile_off, n_tiles, tt):
    i = pl.program_id(0)
    slot = i % 2

    def idx_copy(tile, s):
        return pltpu.make_async_copy(pos_ref.at[tile + tile_off], idx_ref.at[s], sem_idx.at[s])

    def issue_rows(s):
        def body(r, carry):
            row = idx_ref[s, r]
            pltpu.make_async_copy(ys_ref.at[row], ybuf_ref.at[s, r], sem_rows.at[s]).start()
            return carry

        lax.fori_loop(0, TOP_K * tt, body, 0, unroll=8)

    @pl.when(i == 0)
    def _():
        first = idx_copy(0, 0)
        first.start()
        first.wait()
        issue_rows(0)
        if n_tiles > 1:
            idx_copy(1, 1).start()

    @pl.when(i + 1 < n_tiles)
    def _():
        idx_copy(i + 1, 1 - slot).wait()
        issue_rows(1 - slot)

        @pl.when(i + 2 < n_tiles)
        def _():
            idx_copy(i + 2, slot).start()

    buf = ybuf_ref.at[slot]
    pltpu.make_async_copy(buf, buf, sem_rows.at[slot]).wait()
    g = gate_ref[...]
    y0 = ybuf_ref[slot, 0:tt, :]
    y1 = ybuf_ref[slot, tt:2 * tt, :]
    o_ref[...] = x1_ref[...] + (y0 * g[:, 0:1] + y1 * g[:, 1:2])


def _combine(pos_tbl, ys, x1, gate, *, tile_off, tt):
    T, D = x1.shape
    n_tiles = T // tt
    est = 2 * TOP_K * tt * D * 4 + 2 * 2 * tt * D * 4 + 2 * tt * LANES * 4 + 3 * tt * D * 4
    kern = functools.partial(_combine_kernel, tile_off=tile_off, n_tiles=n_tiles, tt=tt)
    return pl.pallas_call(
        kern,
        grid=(n_tiles,),
        in_specs=[
            pl.BlockSpec(memory_space=pl.ANY),
            pl.BlockSpec(memory_space=pl.ANY),
            pl.BlockSpec((tt, D), lambda i: (i, 0)),
            pl.BlockSpec((tt, LANES), lambda i: (i + tile_off, 0)),
        ],
        out_specs=pl.BlockSpec((tt, D), lambda i: (i, 0)),
        out_shape=jax.ShapeDtypeStruct((T, D), F32),
        scratch_shapes=[
            pltpu.SMEM((2, TOP_K * tt), jnp.int32),
            pltpu.VMEM((2, TOP_K * tt, D), F32),
            pltpu.SemaphoreType.DMA((2,)),
            pltpu.SemaphoreType.DMA((2,)),
        ],
        compiler_params=_params(("arbitrary",), est),
        name="moe_combine",
    )(pos_tbl, ys, x1, gate)


def _rope_perm(hd, rot_dim):
    half = rot_dim // 2
    mid = hd // 2
    assert rot_dim <= mid
    perm = np.arange(hd)
    perm[half:rot_dim] = np.arange(mid, mid + half)
    perm[mid:mid + half] = np.arange(half, rot_dim)
    return perm


def _rope_tables(seq, hd, rot_dim):
    half = rot_dim // 2
    mid = hd // 2
    inv_freq = ROPE_THETA ** (-jnp.arange(half, dtype=F32) * 2.0 / rot_dim)
    ang = jnp.arange(seq, dtype=F32)[:, None] * inv_freq[None, :]
    cos, sin = jnp.cos(ang), jnp.sin(ang)
    ones = lambda n: jnp.ones((seq, n), F32)
    zeros = lambda n: jnp.zeros((seq, n), F32)
    cos_t = jnp.concatenate([cos, ones(mid - half), cos, ones(hd - mid - half)], axis=1)
    sx_t = jnp.concatenate([-sin, zeros(mid - half), sin, zeros(hd - mid - half)], axis=1)
    return cos_t, sx_t


def kernel(x_prompt, x_sample, norm_mix, w_in, q_norm, k_norm, lambda_q1, lambda_k1, lambda_q2,
           lambda_k2, subln, w_pool, pool_scale, w_out, norm_ffn, w_router_group, b_router_group,
           w_router_expert, b_router_expert, w_gate, w_up, w_down):
    n_layers, D, in_cols = w_in.shape
    mix = w_out.shape[1]
    aw = (in_cols - mix) // 2
    pw = mix - aw
    hd = q_norm.shape[-1]
    rot_dim = hd // 4
    n_groups = w_router_group.shape[-1]
    n_experts = w_router_expert.shape[-1]
    assert aw % (2 * hd) == 0 and pw % len(POOL_WINDOWS) == 0
    assert n_groups + n_experts <= LANES and (n_experts // n_groups) & (n_experts // n_groups - 1) == 0

    trunks = []
    for x in (x_prompt, x_sample):
        b, s, _ = x.shape
        trunks.append(dict(batch=b, seq=s, x=x.reshape(b * s, D), bands=_pool_bands(min(s, 512))))
    cos_t, sx_t = _rope_tables(max(tr["seq"] for tr in trunks), hd, rot_dim)
    perm = _rope_perm(hd, rot_dim)
    t_a = trunks[0]["x"].shape[0]
    t_b = trunks[1]["x"].shape[0]
    tm_moe = min(512, t_a, t_b)
    tt_cmb = min(256, t_a, t_b)

    for l in range(n_layers):
        lam_init = _lambda_init(l)
        scale = hd ** -0.5 * math.log2(math.e)
        w_qk = w_in[l, :, :2 * aw].reshape(D, 2 * aw // hd, hd)[:, :, perm].reshape(D, 2 * aw)
        w_in_bf = jnp.concatenate([w_qk, w_in[l, :, 2 * aw:]], axis=1).astype(BF16)
        head_gain = jnp.concatenate([jnp.tile(q_norm[l][perm] * scale, aw // hd),
                                     jnp.tile(k_norm[l][perm], aw // hd)]).reshape(1, 2 * aw)
        lamv = jnp.stack([lambda_q1[l], lambda_k1[l], lambda_q2[l], lambda_k2[l]]).astype(F32)
        w_pool_bf = w_pool[l].astype(BF16)
        wa_bf = w_out[l, :aw].astype(BF16)
        wp_bf = w_out[l, aw:].astype(BF16)
        w_r = jnp.concatenate([w_router_group[l], w_router_expert[l]], axis=1)
        w_r = jnp.pad(w_r, ((0, 0), (0, LANES - w_r.shape[1])))
        w_r_hi = w_r.astype(BF16)
        w_r_lo = (w_r - w_r_hi.astype(F32)).astype(BF16)
        b_r = jnp.pad(jnp.concatenate([b_router_group[l], b_router_expert[l]]).astype(F32),
                      (0, LANES - n_groups - n_experts)).reshape(1, LANES)
        dff = w_gate.shape[-1]
        inproj_side = [w_gate[l].reshape(n_experts * D, dff), w_up[l].reshape(n_experts * D, dff)]
        outproj_side = [w_down[l].reshape(n_experts * dff, D), None]
        expert_w = []

        x1s = []
        for tr, side_in, side_out in zip(trunks, inproj_side, outproj_side):
            proj, side_bf = _inproj(tr["x"], norm_mix[l].reshape(1, D), w_in_bf, head_gain, cos_t, sx_t,
                                    side_in, seq=tr["seq"], aw=aw, hd=hd)
            expert_w.append(side_bf.reshape(n_experts, D, dff))
            a = _attention(proj, lamv, subln[l].reshape(1, 2 * hd), batch=tr["batch"], seq=tr["seq"],
                           aw=aw, hd=hd, lam_init=lam_init)
            p = _pool(proj, w_pool_bf, pool_scale[l].reshape(1, pw), tr["bands"], batch=tr["batch"],
                      seq=tr["seq"], aw=aw, pw=pw)
            if side_out is None:
                x1s.append(_outproj(a, p, wa_bf, wp_bf, tr["x"]))
            else:
                x1, side_bf = _outproj(a, p, wa_bf, wp_bf, tr["x"], side_out)
                x1s.append(x1)
                expert_w.append(side_bf.reshape(n_experts, dff, D))
        wg_bf, wd_bf, wu_bf = expert_w

        xn, eid, gate = _router(x1s[0], x1s[1], norm_ffn[l].reshape(1, D), w_r_hi, w_r_lo, b_r,
                                n_groups=n_groups, n_experts=n_experts)
        te, tlen, n_tiles, tok_tbl, pos, _ = _moe_plan(eid[:, :TOP_K], tm=tm_moe, n_experts=n_experts)
        ys = _experts(te, n_tiles, tlen, tok_tbl, xn, wg_bf, wu_bf, wd_bf, tm=tm_moe)
        t_all = t_a + t_b
        pos_tbl = pos.reshape(t_all // tt_cmb, tt_cmb, TOP_K).transpose(0, 2, 1).reshape(
            t_all // tt_cmb, TOP_K * tt_cmb)
        outs = []
        tile_off = 0
        for tr, x1 in zip(trunks, x1s):
            outs.append(_combine(pos_tbl, ys, x1, gate, tile_off=tile_off, tt=tt_cmb))
            tile_off += x1.shape[0] // tt_cmb
        for tr, o in zip(trunks, outs):
            tr["x"] = o

    return tuple(tr["x"].reshape(tr["batch"], tr["seq"], D) for tr in trunks)
```

```python
import functools
import math

import jax
import jax.numpy as jnp
import numpy as np
from jax import lax
from jax.experimental import pallas as pl
from jax.experimental.pallas import tpu as pltpu

EPS = 1e-6
ROPE_THETA = 500000.0
POOL_WINDOWS = (2, 4, 8, 16)
TOP_K = 2
POOL_HALO = 16
LANES = 128
VMEM_CAP_BYTES = 60000 * 1024
F32 = jnp.float32
BF16 = jnp.bfloat16


def _lambda_init(layer_idx):
    return 0.8 - 0.6 * math.exp(-0.3 * layer_idx)


def _vmem_limit(est_bytes):
    return int(min(VMEM_CAP_BYTES, est_bytes * 5 // 4 + (4 << 20)))


def _params(sem, est_bytes):
    return pltpu.CompilerParams(dimension_semantics=sem, vmem_limit_bytes=_vmem_limit(est_bytes))


def _cast_block_rows(rows, steps):
    rb = -(-rows // steps)
    while rows % rb or rb % 16:
        rb += 1
    return rb


def _inproj_kernel(x_ref, g_ref, w_ref, hg_ref, cos_ref, sx_ref, wc_ref, o_ref, wc_out_ref,
                   xn_ref, acc0_ref, acc1_ref, *, n_qk_tiles, n_col_tiles, hd):
    j = pl.program_id(1)
    tn = w_ref.shape[1]
    accs = (acc0_ref, acc1_ref)

    def cast_side():
        wc_out_ref[...] = wc_ref[...].astype(wc_out_ref.dtype)

    @pl.when(j == 0)
    def _():
        x = x_ref[...]
        ms = jnp.mean(x * x, axis=-1, keepdims=True)
        xn_ref[...] = (x * lax.rsqrt(ms + EPS) * g_ref[...]).astype(xn_ref.dtype)

    def matmul_into(acc_ref):
        acc_ref[...] = jnp.dot(xn_ref[...], w_ref[...], preferred_element_type=F32)

    def qk_epilogue(acc_ref):
        cos = cos_ref[...]
        sx = sx_ref[...]
        for c in range(tn // hd):
            blk = acc_ref[:, c * hd:(c + 1) * hd]
            ms = jnp.mean(blk * blk, axis=-1, keepdims=True)
            y = blk * lax.rsqrt(ms + EPS) * hg_ref[:, c * hd:(c + 1) * hd]
            y = y * cos + pltpu.roll(y, hd // 2, 1) * sx
            o_ref[:, c * hd:(c + 1) * hd] = y.astype(o_ref.dtype)

    def plain_epilogue(acc_ref):
        o_ref[...] = acc_ref[...].astype(o_ref.dtype)

    @pl.when(j == 0)
    def _():
        cast_side()
        matmul_into(accs[0])

    for parity in range(2):
        cur, prev = accs[parity], accs[1 - parity]
        is_par = (j % 2) == parity

        @pl.when(jnp.logical_and(is_par, jnp.logical_and(j >= 1, j <= n_qk_tiles)))
        def _(cur=cur, prev=prev):
            cast_side()
            matmul_into(cur)
            qk_epilogue(prev)

        @pl.when(jnp.logical_and(is_par, jnp.logical_and(j > n_qk_tiles, j < n_col_tiles)))
        def _(cur=cur, prev=prev):
            cast_side()
            matmul_into(cur)
            plain_epilogue(prev)

    @pl.when(j == n_col_tiles)
    def _():
        cast_side()
        plain_epilogue(accs[(n_col_tiles - 1) % 2])


def _inproj(x2, gain, w_bf, head_gain, cos_t, sx_t, side_w, *, seq, aw, hd):
    T, D = x2.shape
    N = w_bf.shape[1]
    tm = min(512, seq)
    tn = min(512, 2 * aw)
    n_qk_tiles = (2 * aw) // tn
    n_col_tiles = N // tn
    assert n_qk_tiles < n_col_tiles
    s_tiles = seq // tm
    n_steps = n_col_tiles + 1
    rows, cols = side_w.shape
    rb = _cast_block_rows(rows, (T // tm) * n_steps)
    cast_blocks = rows // rb
    cast_ix = lambda i, j: (jnp.minimum(i * n_steps + j, cast_blocks - 1), 0)
    est = (2 * tm * D * 4 + 2 * D * tn * 2 + tm * D * 2 + 2 * tm * tn * 2 + 4 * tm * tn * 4
           + 2 * rb * cols * (4 + 2))
    kern = functools.partial(_inproj_kernel, n_qk_tiles=n_qk_tiles, n_col_tiles=n_col_tiles, hd=hd)
    return pl.pallas_call(
        kern,
        grid=(T // tm, n_steps),
        in_specs=[
            pl.BlockSpec((tm, D), lambda i, j: (i, 0)),
            pl.BlockSpec((1, D), lambda i, j: (0, 0)),
            pl.BlockSpec((D, tn), lambda i, j: (0, jnp.minimum(j, n_col_tiles - 1))),
            pl.BlockSpec((1, tn), lambda i, j: (0, jnp.clip(j - 1, 0, n_qk_tiles - 1))),
            pl.BlockSpec((tm, hd), lambda i, j: (i % s_tiles, 0)),
            pl.BlockSpec((tm, hd), lambda i, j: (i % s_tiles, 0)),
            pl.BlockSpec((rb, cols), cast_ix),
        ],
        out_specs=[
            pl.BlockSpec((tm, tn), lambda i, j: (i, jnp.maximum(j - 1, 0))),
            pl.BlockSpec((rb, cols), cast_ix),
        ],
        out_shape=[jax.ShapeDtypeStruct((T, N), BF16), jax.ShapeDtypeStruct((rows, cols), BF16)],
        scratch_shapes=[pltpu.VMEM((tm, D), BF16), pltpu.VMEM((tm, tn), F32), pltpu.VMEM((tm, tn), F32)],
        compiler_params=_params(("arbitrary", "arbitrary"), est),
        name="inproj",
    )(x2, gain, w_bf, head_gain, cos_t, sx_t, side_w)


def _attn_kernel(lam_ref, subln_ref, q_ref, k_ref, v_ref, o_ref,
                 vt_ref, e_ref, mc_ref, lc_ref, acc_ref, *, tk, sub, hd, lam_init, unroll):
    seq = k_ref.shape[0]
    nck = seq // tk
    contract_last = (((1,), (1,)), ((), ()))

    @pl.when(pl.program_id(2) == 0)
    def _():
        def transpose_v(c, carry):
            vc = v_ref[pl.ds(pl.multiple_of(c * tk, tk), tk), :]
            vt_ref[c] = vc.astype(F32).T.astype(vt_ref.dtype)
            return carry

        lax.fori_loop(0, nck, transpose_v, 0)

    q = q_ref[...]
    tq = q.shape[0]
    nsub = tk // sub
    sl = 8

    def pass1(c, carry):
        kc = k_ref[pl.ds(pl.multiple_of(c * tk, tk), tk), :]
        for comp in range(2):
            st = lax.dot_general(kc[:, comp * hd:(comp + 1) * hd], q[:, comp * hd:(comp + 1) * hd],
                                 contract_last, preferred_element_type=F32)
            for j in range(nsub):
                sj = st[j * sub:(j + 1) * sub].reshape(sub // sl, sl, tq)
                mj = jnp.max(sj, axis=0)
                ej = jnp.exp2(sj - mj[None])
                mc_ref[comp, c, j] = mj
                lc_ref[comp, c, j] = jnp.sum(ej, axis=0)
                e_ref[comp, c, j * sub:(j + 1) * sub, :] = ej.reshape(sub, tq).astype(e_ref.dtype)
        return carry

    lax.fori_loop(0, nck, pass1, 0, unroll=unroll)

    lv = lam_ref[...]
    lam = (jnp.exp(jnp.sum(lv[0:1] * lv[1:2], axis=1, keepdims=True))
           - jnp.exp(jnp.sum(lv[2:3] * lv[3:4], axis=1, keepdims=True)) + lam_init)
    def over_blocks(red, x):
        return red(red(red(x, axis=0), axis=0), axis=0, keepdims=True)

    m0 = over_blocks(jnp.max, mc_ref[0])
    m1 = over_blocks(jnp.max, mc_ref[1])
    r0 = 1.0 / over_blocks(jnp.sum, lc_ref[0] * jnp.exp2(mc_ref[0] - m0))
    r1 = lam / over_blocks(jnp.sum, lc_ref[1] * jnp.exp2(mc_ref[1] - m1))
    acc_ref[...] = jnp.zeros(acc_ref.shape, F32)

    def pass2(c, carry):
        f0 = jnp.exp2(mc_ref[0, c] - m0) * r0
        f1 = jnp.exp2(mc_ref[1, c] - m1) * r1
        pk = 2 * sl
        blocks = []
        for j in range(nsub):
            g0 = jnp.concatenate([f0[j], f0[j]], axis=0).astype(e_ref.dtype)
            g1 = jnp.concatenate([f1[j], f1[j]], axis=0).astype(e_ref.dtype)
            e0 = e_ref[0, c, j * sub:(j + 1) * sub, :].reshape(sub // pk, pk, tq)
            e1 = e_ref[1, c, j * sub:(j + 1) * sub, :].reshape(sub // pk, pk, tq)
            blocks.append((e0 * g0[None] - e1 * g1[None]).reshape(sub, tq))
        wt = jnp.concatenate(blocks, axis=0)
        acc_ref[...] += jnp.dot(vt_ref[c], wt, preferred_element_type=F32)
        return carry

    lax.fori_loop(0, nck, pass2, 0, unroll=unroll)

    o = acc_ref[...].T
    ms = jnp.mean(o * o, axis=-1, keepdims=True)
    o = o * lax.rsqrt(ms + EPS) * subln_ref[...]
    o_ref[...] = (o * (1.0 - lam_init)).astype(o_ref.dtype)


def _attention(proj, lamv, subln, *, batch, seq, aw, hd, lam_init):
    T = proj.shape[0]
    vd = 2 * hd
    n_heads = aw // vd
    tq = min(seq, 512)
    tk = min(seq, 512)
    nck = seq // tk
    q_tiles = seq // tq
    k_col0 = aw // vd
    v_col0 = 2 * aw // vd
    sub = min(tk, 128)
    nsub = tk // sub
    est = (2 * 2 * seq * vd * 2 + seq * vd * 2 + 2 * seq * tq * 2 + 2 * 2 * nck * nsub * 8 * tq * 4
           + tq * vd * 4 + 4 * tq * vd * 2 + 6 * tq * tk * 4)
    kern = functools.partial(_attn_kernel, tk=tk, sub=sub, hd=hd, lam_init=lam_init, unroll=min(8, nck))
    return pl.pallas_call(
        kern,
        grid=(batch, n_heads, q_tiles),
        in_specs=[
            pl.BlockSpec((4, hd), lambda b, h, i: (0, 0)),
            pl.BlockSpec((1, vd), lambda b, h, i: (0, 0)),
            pl.BlockSpec((tq, vd), lambda b, h, i: (b * q_tiles + i, h)),
            pl.BlockSpec((seq, vd), lambda b, h, i: (b, k_col0 + h)),
            pl.BlockSpec((seq, vd), lambda b, h, i: (b, v_col0 + h)),
        ],
        out_specs=pl.BlockSpec((tq, vd), lambda b, h, i: (b * q_tiles + i, h)),
        out_shape=jax.ShapeDtypeStruct((T, aw), BF16),
        scratch_shapes=[
            pltpu.VMEM((nck, vd, tk), BF16),
            pltpu.VMEM((2, nck, tk, tq), BF16),
            pltpu.VMEM((2, nck, nsub, 8, tq), F32),
            pltpu.VMEM((2, nck, nsub, 8, tq), F32),
            pltpu.VMEM((vd, tq), F32),
        ],
        compiler_params=_params(("arbitrary", "arbitrary", "arbitrary"), est),
        name="diff_attention",
    )(lamv, subln, proj, proj, proj)


def _pool_bands(rows):
    t = np.arange(rows)[:, None]
    jm = np.arange(rows)[None, :]
    jh = np.arange(POOL_HALO)[None, :]
    bm, bp, bn = [], [], []
    for w in POOL_WINDOWS:
        left = w // 2
        right = w - 1 - left
        bm.append((jm >= t - left) & (jm <= t + right))
        bp.append(jh - POOL_HALO >= t - left)
        bn.append(rows + jh <= t + right)
    as_bf = lambda m: jnp.asarray(np.stack(m).astype(np.float32), dtype=BF16)
    return as_bf(bm), as_bf(bp), as_bf(bn)


def _pool_kernel(u_ref, wp_ref, ps_ref, bm_ref, bp_ref, bn_ref, o_ref, *, rows):
    g = pl.program_id(1)
    seq = u_ref.shape[0]
    n_chunks = seq // rows
    for gi, w in enumerate(POOL_WINDOWS):
        left = w // 2
        right = w - 1 - left

        @pl.when(g == gi)
        def _(gi=gi, left=left, right=right):
            def body(r, carry):
                r0 = pl.multiple_of(r * rows, rows)
                main = u_ref[pl.ds(r0, rows), :]
                p0 = pl.multiple_of(jnp.maximum(r0 - POOL_HALO, 0), POOL_HALO)
                n0 = pl.multiple_of(jnp.minimum(r0 + rows, seq - POOL_HALO), POOL_HALO)
                prev = u_ref[pl.ds(p0, POOL_HALO), :]
                nxt = u_ref[pl.ds(n0, POOL_HALO), :]
                prev = jnp.where(r > 0, prev, jnp.zeros_like(prev))
                nxt = jnp.where(r < n_chunks - 1, nxt, jnp.zeros_like(nxt))
                ssum = (jnp.dot(bm_ref[gi], main, preferred_element_type=F32)
                        + jnp.dot(bp_ref[gi], prev, preferred_element_type=F32)
                        + jnp.dot(bn_ref[gi], nxt, preferred_element_type=F32))
                t = r0 + lax.broadcasted_iota(jnp.int32, (rows, 1), 0)
                lo = jnp.maximum(t - left, 0)
                hi = jnp.minimum(t + right, seq - 1)
                cnt = (hi - lo + 1).astype(F32)
                delta = ssum / cnt - main.astype(F32)
                y = jnp.dot(delta.astype(wp_ref.dtype), wp_ref[...], preferred_element_type=F32)
                o_ref[pl.ds(r0, rows), :] = (y * ps_ref[...]).astype(o_ref.dtype)
                return carry

            lax.fori_loop(0, n_chunks, body, 0)


def _pool(proj, w_pool_bf, pool_scale, bands, *, batch, seq, aw, pw):
    T = proj.shape[0]
    n_groups, gw, _ = w_pool_bf.shape
    rows = min(seq, 512)
    u_col0 = 3 * aw // gw
    bm, bp, bn = bands
    est = (2 * 2 * seq * gw * 2 + 2 * gw * gw * 2 + 2 * 4 * rows * (rows + 2 * POOL_HALO) * 2
           + 8 * rows * gw * 4)
    kern = functools.partial(_pool_kernel, rows=rows)
    return pl.pallas_call(
        kern,
        grid=(batch, n_groups),
        in_specs=[
            pl.BlockSpec((seq, gw), lambda b, g: (b, u_col0 + g)),
            pl.BlockSpec((None, gw, gw), lambda b, g: (g, 0, 0)),
            pl.BlockSpec((1, gw), lambda b, g: (0, g)),
            pl.BlockSpec(bm.shape, lambda b, g: (0, 0, 0)),
            pl.BlockSpec(bp.shape, lambda b, g: (0, 0, 0)),
            pl.BlockSpec(bn.shape, lambda b, g: (0, 0, 0)),
        ],
        out_specs=pl.BlockSpec((seq, gw), lambda b, g: (b, g)),
        out_shape=jax.ShapeDtypeStruct((T, pw), BF16),
        compiler_params=_params(("arbitrary", "arbitrary"), est),
        name="multiscale_pool",
    )(proj, w_pool_bf, pool_scale, bm, bp, bn)


def _outproj_kernel(a_ref, p_ref, wa_ref, wp_ref, x_ref, *rest):
    if len(rest) == 3:
        wc_ref, o_ref, wc_out_ref = rest
        wc_out_ref[...] = wc_ref[...].astype(wc_out_ref.dtype)
    else:
        (o_ref,) = rest
    acc = (jnp.dot(a_ref[...], wa_ref[...], preferred_element_type=F32)
           + jnp.dot(p_ref[...], wp_ref[...], preferred_element_type=F32))
    o_ref[...] = x_ref[...] + acc


def _outproj(a, p, wa_bf, wp_bf, x2, side_w=None):
    T, D = x2.shape
    aw = a.shape[1]
    pw = p.shape[1]
    tm = min(512, T)
    tn = min(1024, D)
    n_col = D // tn
    est = 2 * (tm * aw * 2 + tm * pw * 2 + (aw + pw) * tn * 2 + 2 * tm * tn * 4) + 2 * tm * tn * 4
    in_specs = [
        pl.BlockSpec((tm, aw), lambda i, j: (i, 0)),
        pl.BlockSpec((tm, pw), lambda i, j: (i, 0)),
        pl.BlockSpec((aw, tn), lambda i, j: (0, j)),
        pl.BlockSpec((pw, tn), lambda i, j: (0, j)),
        pl.BlockSpec((tm, tn), lambda i, j: (i, j)),
    ]
    out_specs = [pl.BlockSpec((tm, tn), lambda i, j: (i, j))]
    out_shape = [jax.ShapeDtypeStruct((T, D), F32)]
    args = [a, p, wa_bf, wp_bf, x2]
    if side_w is not None:
        rows, cols = side_w.shape
        rb = _cast_block_rows(rows, (T // tm) * n_col)
        cast_blocks = rows // rb
        cast_ix = lambda i, j: (jnp.minimum(i * n_col + j, cast_blocks - 1), 0)
        in_specs.append(pl.BlockSpec((rb, cols), cast_ix))
        out_specs.append(pl.BlockSpec((rb, cols), cast_ix))
        out_shape.append(jax.ShapeDtypeStruct((rows, cols), BF16))
        args.append(side_w)
        est += 2 * rb * cols * (4 + 2)
    res = pl.pallas_call(
        _outproj_kernel,
        grid=(T // tm, n_col),
        in_specs=in_specs,
        out_specs=out_specs,
        out_shape=out_shape,
        compiler_params=_params(("arbitrary", "arbitrary"), est),
        name="outproj",
    )(*args)
    return res if side_w is not None else res[0]


def _router_kernel(xa_ref, xb_ref, g_ref, whi_ref, wlo_ref, b_ref, xn_ref, eid_ref, gate_ref, *,
                   a_tiles, n_groups, n_experts):
    i = pl.program_id(0)
    epg_shift = int(math.log2(n_experts // n_groups))

    def run(x_ref):
        x = x_ref[...]
        ms = jnp.mean(x * x, axis=-1, keepdims=True)
        xn = x * lax.rsqrt(ms + EPS) * g_ref[...]
        hi = xn.astype(BF16)
        xn_ref[...] = hi.astype(F32)
        lo = (xn - hi.astype(F32)).astype(BF16)
        whi = whi_ref[...]
        logits = (jnp.dot(hi, whi, preferred_element_type=F32)
                  + jnp.dot(lo, whi, preferred_element_type=F32)
                  + jnp.dot(hi, wlo_ref[...], preferred_element_type=F32)
                  + b_ref[...])
        lane = lax.broadcasted_iota(jnp.int32, logits.shape, 1)
        neg = jnp.float32(-jnp.inf)
        is_g = lane < n_groups
        gl = jnp.where(is_g, logits, neg)
        gmax = jnp.max(gl, axis=1, keepdims=True)
        g_idx = jnp.min(jnp.where(gl == gmax, lane, LANES), axis=1, keepdims=True)
        gsum = jnp.sum(jnp.where(is_g, jnp.exp(gl - gmax), 0.0), axis=1, keepdims=True)
        g_w = 1.0 / gsum
        e_lane = lane - n_groups
        grp = jnp.where(jnp.logical_and(e_lane >= 0, e_lane < n_experts),
                        lax.shift_right_arithmetic(e_lane, epg_shift), -1)
        sel = grp == g_idx
        el = jnp.where(sel, logits, neg)
        emax = jnp.max(el, axis=1, keepdims=True)
        ex = jnp.where(sel, jnp.exp(el - emax), 0.0)
        prob = ex / jnp.sum(ex, axis=1, keepdims=True)
        pm = jnp.where(sel, prob, -1.0)
        p1 = jnp.max(pm, axis=1, keepdims=True)
        i1 = jnp.min(jnp.where(pm == p1, lane, LANES), axis=1, keepdims=True)
        pm2 = jnp.where(lane == i1, -1.0, pm)
        p2 = jnp.max(pm2, axis=1, keepdims=True)
        i2 = jnp.min(jnp.where(pm2 == p2, lane, LANES), axis=1, keepdims=True)
        tsum = p1 + p2
        gate1 = g_w * (p1 / tsum)
        gate2 = g_w * (p2 / tsum)
        eid_ref[...] = jnp.where(lane == 0, i1 - n_groups, jnp.where(lane == 1, i2 - n_groups, 0))
        gate_ref[...] = jnp.where(lane == 0, gate1, jnp.where(lane == 1, gate2, 0.0))

    @pl.when(i < a_tiles)
    def _():
        run(xa_ref)

    @pl.when(i >= a_tiles)
    def _():
        run(xb_ref)


def _router(x1a, x1b, gain, whi, wlo, bias, *, n_groups, n_experts):
    Ta, D = x1a.shape
    Tb = x1b.shape[0]
    tt = min(256, Ta, Tb)
    a_tiles = Ta // tt
    b_tiles = Tb // tt
    T = Ta + Tb
    est = 2 * 2 * tt * D * 4 + 2 * tt * D * 4 + 4 * D * LANES * 2 + 6 * tt * D * 4
    kern = functools.partial(_router_kernel, a_tiles=a_tiles, n_groups=n_groups, n_experts=n_experts)
    return pl.pallas_call(
        kern,
        grid=(a_tiles + b_tiles,),
        in_specs=[
            pl.BlockSpec((tt, D), lambda i: (jnp.minimum(i, a_tiles - 1), 0)),
            pl.BlockSpec((tt, D), lambda i: (jnp.maximum(i - a_tiles, 0), 0)),
            pl.BlockSpec((1, D), lambda i: (0, 0)),
            pl.BlockSpec((D, LANES), lambda i: (0, 0)),
            pl.BlockSpec((D, LANES), lambda i: (0, 0)),
            pl.BlockSpec((1, LANES), lambda i: (0, 0)),
        ],
        out_specs=[
            pl.BlockSpec((tt, D), lambda i: (i, 0)),
            pl.BlockSpec((tt, LANES), lambda i: (i, 0)),
            pl.BlockSpec((tt, LANES), lambda i: (i, 0)),
        ],
        out_shape=[
            jax.ShapeDtypeStruct((T, D), F32),
            jax.ShapeDtypeStruct((T, LANES), jnp.int32),
            jax.ShapeDtypeStruct((T, LANES), F32),
        ],
        compiler_params=_params(("arbitrary",), est),
        name="router",
    )(x1a, x1b, gain, whi, wlo, bias)


def _moe_plan(eid2, *, tm, n_experts):
    T = eid2.shape[0]
    A = T * TOP_K
    eflat = eid2.reshape(A)
    experts = jnp.arange(n_experts, dtype=jnp.int32)
    sorted_e, order = lax.sort_key_val(eflat, jnp.arange(A, dtype=jnp.int32))
    counts = jnp.sum((eflat[None, :] == experts[:, None]).astype(jnp.int32), axis=1)
    ends = jnp.cumsum(counts).astype(jnp.int32)
    starts = ends - counts
    ntile_e = (counts + tm - 1) // tm
    tile_end = jnp.cumsum(ntile_e).astype(jnp.int32)
    tile_base = tile_end - ntile_e
    n_tiles = tile_end[-1]
    G = -(-A // tm) + n_experts
    ti = jnp.arange(G, dtype=jnp.int32)
    valid = ti < n_tiles
    te = jnp.sum((ti[:, None] >= tile_end[None, :]).astype(jnp.int32), axis=1)
    te = jnp.minimum(te, n_experts - 1)
    te = jnp.where(valid, te, te[n_tiles - 1])
    j = ti - tile_base[te]
    tstart = starts[te] + j * tm
    tlen = jnp.where(valid, jnp.clip(counts[te] - j * tm, 0, tm), 0).astype(jnp.int32)
    r = jnp.arange(tm, dtype=jnp.int32)
    sidx = jnp.clip(tstart[:, None] + r[None, :], 0, A - 1)
    tok_tbl = jnp.where(r[None, :] < tlen[:, None], order[sidx] // TOP_K, 0).astype(jnp.int32)
    rank = jnp.arange(A, dtype=jnp.int32) - starts[sorted_e]
    row = tile_base[sorted_e] * tm + rank
    _, pos = lax.sort_key_val(order, row)
    pos = pos.reshape(T, TOP_K)
    return te, tlen, n_tiles.reshape(1), tok_tbl, pos, G


def _expert_kernel(te_ref, nt_ref, tl_ref, tbl_ref, xn_ref, wg_ref, wu_ref, wd_ref, o_ref,
                   idx_ref, xrow_ref, h_ref, sem_idx, sem_rows, *,
                   n_tiles, fa, tf, half, grp):
    del te_ref, nt_ref
    i = pl.program_id(0)
    s = pl.program_id(1)
    tm = xrow_ref.shape[0]
    n_grp = tm // grp
    ln = tl_ref[i]

    def idx_copy(tile):
        dst = idx_ref.at[pl.ds(pl.multiple_of((tile % 2) * tm, tm), tm)]
        return pltpu.make_async_copy(tbl_ref.at[tile], dst, sem_idx.at[tile % 2])

    def issue_rows(tile):
        cnt = tl_ref[tile]
        base = (tile % 2) * tm
        for g in range(n_grp):
            @pl.when(g * grp < cnt)
            def _(g=g):
                for r in range(g * grp, (g + 1) * grp):
                    tok = idx_ref[base + r]
                    pltpu.make_async_copy(xn_ref.at[tok], xrow_ref.at[r], sem_rows).start()

    def wait_rows():
        for g in range(n_grp):
            @pl.when(g * grp < ln)
            def _(g=g):
                blk = xrow_ref.at[pl.ds(g * grp, grp)]
                pltpu.make_async_copy(blk, blk, sem_rows).wait()

    @pl.when(jnp.logical_and(i == 0, s == 0))
    def _():
        xrow_ref[...] = jnp.zeros(xrow_ref.shape, xrow_ref.dtype)
        first = idx_copy(0)
        first.start()
        first.wait()
        n_first = ((tl_ref[0] + grp - 1) // grp) * grp

        def issue_first(r, carry):
            pltpu.make_async_copy(xn_ref.at[idx_ref[r]], xrow_ref.at[r], sem_rows).start()
            return carry

        lax.fori_loop(0, n_first, issue_first, 0)
        if n_tiles > 1:
            idx_copy(1).start()

    @pl.when(s == 0)
    def _():
        wait_rows()

    @pl.when(jnp.logical_and(s == fa, i + 1 < n_tiles))
    def _():
        idx_copy(i + 1).wait()
        issue_rows(i + 1)

        @pl.when(i + 2 < n_tiles)
        def _():
            idx_copy(i + 2).start()

    def rows_variants(fn):
        @pl.when(ln > half)
        def _():
            fn(tm)

        @pl.when(jnp.logical_and(ln > 0, ln <= half))
        def _():
            fn(half)

    @pl.when(s < fa)
    def _():
        def gate_up(nrows):
            x = xrow_ref[0:nrows, :].astype(wg_ref.dtype)
            gate = jnp.dot(x, wg_ref[...], preferred_element_type=F32)
            up = jnp.dot(x, wu_ref[...], preferred_element_type=F32)
            h_ref[s, 0:nrows, :] = (jax.nn.silu(gate) * up).astype(h_ref.dtype)

        rows_variants(gate_up)

    @pl.when(s >= fa)
    def _():
        def down(nrows):
            y = jnp.dot(h_ref[0, 0:nrows, :], wd_ref[0:tf, :], preferred_element_type=F32)
            for f in range(1, fa):
                y += jnp.dot(h_ref[f, 0:nrows, :], wd_ref[f * tf:(f + 1) * tf, :],
                             preferred_element_type=F32)
            o_ref[0:nrows, :] = y

        rows_variants(down)

        @pl.when(ln <= half)
        def _():
            o_ref[half:tm, :] = jnp.zeros((tm - half, o_ref.shape[1]), o_ref.dtype)

        @pl.when(ln == 0)
        def _():
            o_ref[0:half, :] = jnp.zeros((half, o_ref.shape[1]), o_ref.dtype)


def _experts(te, n_tiles, tlen, tok_tbl, xn, wg_bf, wu_bf, wd_bf, *, tm):
    G = te.shape[0]
    E, D, dff = wg_bf.shape
    tf = min(512, dff)
    fa = dff // tf
    tn = min(2048, D)
    fb = D // tn
    half = tm // 2
    grp = min(64, tm)
    est = (tm * D * 4 + tm * D * 2 + tm * dff * 2 + 2 * 2 * D * tf * 2 + 2 * dff * tn * 2
           + 2 * tm * tn * 4 + 3 * tm * tf * 4 + tm * tn * 4)

    def a_ix(i, s, nt):
        return jnp.where(i < nt[0], jnp.minimum(s, fa - 1), fa - 1)

    def b_ix(s):
        return jnp.clip(s - fa, 0, fb - 1)

    def wd_ix(i, s, nt):
        return jnp.where(i < nt[0], b_ix(s), fb - 1)

    kern = functools.partial(_expert_kernel, n_tiles=G, fa=fa, tf=tf, half=half, grp=grp)
    return pl.pallas_call(
        kern,
        grid_spec=pltpu.PrefetchScalarGridSpec(
            num_scalar_prefetch=3,
            grid=(G, fa + fb),
            in_specs=[
                pl.BlockSpec(memory_space=pl.ANY),
                pl.BlockSpec(memory_space=pl.ANY),
                pl.BlockSpec((None, D, tf), lambda i, s, te, nt, tl: (te[i], 0, a_ix(i, s, nt))),
                pl.BlockSpec((None, D, tf), lambda i, s, te, nt, tl: (te[i], 0, a_ix(i, s, nt))),
                pl.BlockSpec((None, dff, tn), lambda i, s, te, nt, tl: (te[i], 0, wd_ix(i, s, nt))),
            ],
            out_specs=pl.BlockSpec((tm, tn), lambda i, s, te, nt, tl: (i, b_ix(s))),
            scratch_shapes=[
                pltpu.SMEM((2 * tm,), jnp.int32),
                pltpu.VMEM((tm, D), F32),
                pltpu.VMEM((fa, tm, tf), BF16),
                pltpu.SemaphoreType.DMA((2,)),
                pltpu.SemaphoreType.DMA(()),
            ],
        ),
        out_shape=jax.ShapeDtypeStruct((G * tm, D), F32),
        compiler_params=_params(("arbitrary", "arbitrary"), est),
        name="moe_experts",
    )(te, n_tiles, tlen, tok_tbl, xn, wg_bf, wu_bf, wd_bf)


def _combine_kernel(pos_ref, ys_ref, x1_ref, gate_ref, o_ref, idx_ref, ybuf_ref, sem_idx, sem_rows, *,
                    tile_off, n_tiles, tt):
    i = pl.program_id(0)
    slot = i % 2

    def idx_copy(tile, s):
        return pltpu.make_async_copy(pos_ref.at[tile + tile_off], idx_ref.at[s], sem_idx.at[s])

    def issue_rows(s):
        def body(r, carry):
            row = idx_ref[s, r]
            pltpu.make_async_copy(ys_ref.at[row], ybuf_ref.at[s, r], sem_rows.at[s]).start()
            return carry

        lax.fori_loop(0, TOP_K * tt, body, 0, unroll=8)

    @pl.when(i == 0)
    def _():
        first = idx_copy(0, 0)
        first.start()
        first.wait()
        issue_rows(0)
        if n_tiles > 1:
            idx_copy(1, 1).start()

    @pl.when(i + 1 < n_tiles)
    def _():
        idx_copy(i + 1, 1 - slot).wait()
        issue_rows(1 - slot)

        @pl.when(i + 2 < n_tiles)
        def _():
            idx_copy(i + 2, slot).start()

    buf = ybuf_ref.at[slot]
    pltpu.make_async_copy(buf, buf, sem_rows.at[slot]).wait()
    g = gate_ref[...]
    y0 = ybuf_ref[slot, 0:tt, :]
    y1 = ybuf_ref[slot, tt:2 * tt, :]
    o_ref[...] = x1_ref[...] + (y0 * g[:, 0:1] + y1 * g[:, 1:2])


def _combine(pos_tbl, ys, x1, gate, *, tile_off, tt):
    T, D = x1.shape
    n_tiles = T // tt
    est = 2 * TOP_K * tt * D * 4 + 2 * 2 * tt * D * 4 + 2 * tt * LANES * 4 + 3 * tt * D * 4
    kern = functools.partial(_combine_kernel, tile_off=tile_off, n_tiles=n_tiles, tt=tt)
    return pl.pallas_call(
        kern,
        grid=(n_tiles,),
        in_specs=[
            pl.BlockSpec(memory_space=pl.ANY),
            pl.BlockSpec(memory_space=pl.ANY),
            pl.BlockSpec((tt, D), lambda i: (i, 0)),
            pl.BlockSpec((tt, LANES), lambda i: (i + tile_off, 0)),
        ],
        out_specs=pl.BlockSpec((tt, D), lambda i: (i, 0)),
        out_shape=jax.ShapeDtypeStruct((T, D), F32),
        scratch_shapes=[
            pltpu.SMEM((2, TOP_K * tt), jnp.int32),
            pltpu.VMEM((2, TOP_K * tt, D), F32),
            pltpu.SemaphoreType.DMA((2,)),
            pltpu.SemaphoreType.DMA((2,)),
        ],
        compiler_params=_params(("arbitrary",), est),
        name="moe_combine",
    )(pos_tbl, ys, x1, gate)


def _rope_perm(hd, rot_dim):
    half = rot_dim // 2
    mid = hd // 2
    assert rot_dim <= mid
    perm = np.arange(hd)
    perm[half:rot_dim] = np.arange(mid, mid + half)
    perm[mid:mid + half] = np.arange(half, rot_dim)
    return perm


def _rope_tables(seq, hd, rot_dim):
    half = rot_dim // 2
    mid = hd // 2
    inv_freq = ROPE_THETA ** (-jnp.arange(half, dtype=F32) * 2.0 / rot_dim)
    ang = jnp.arange(seq, dtype=F32)[:, None] * inv_freq[None, :]
    cos, sin = jnp.cos(ang), jnp.sin(ang)
    ones = lambda n: jnp.ones((seq, n), F32)
    zeros = lambda n: jnp.zeros((seq, n), F32)
    cos_t = jnp.concatenate([cos, ones(mid - half), cos, ones(hd - mid - half)], axis=1)
    sx_t = jnp.concatenate([-sin, zeros(mid - half), sin, zeros(hd - mid - half)], axis=1)
    return cos_t, sx_t


def kernel(x_prompt, x_sample, norm_mix, w_in, q_norm, k_norm, lambda_q1, lambda_k1, lambda_q2,
           lambda_k2, subln, w_pool, pool_scale, w_out, norm_ffn, w_router_group, b_router_group,
           w_router_expert, b_router_expert, w_gate, w_up, w_down):
    n_layers, D, in_cols = w_in.shape
    mix = w_out.shape[1]
    aw = (in_cols - mix) // 2
    pw = mix - aw
    hd = q_norm.shape[-1]
    rot_dim = hd // 4
    n_groups = w_router_group.shape[-1]
    n_experts = w_router_expert.shape[-1]
    assert aw % (2 * hd) == 0 and pw % len(POOL_WINDOWS) == 0
    assert n_groups + n_experts <= LANES and (n_experts // n_groups) & (n_experts // n_groups - 1) == 0

    trunks = []
    for x in (x_prompt, x_sample):
        b, s, _ = x.shape
        trunks.append(dict(batch=b, seq=s, x=x.reshape(b * s, D), bands=_pool_bands(min(s, 512))))
    cos_t, sx_t = _rope_tables(max(tr["seq"] for tr in trunks), hd, rot_dim)
    perm = _rope_perm(hd, rot_dim)
    t_a = trunks[0]["x"].shape[0]
    t_b = trunks[1]["x"].shape[0]
    tm_moe = min(512, t_a, t_b)
    tt_cmb = min(256, t_a, t_b)

    for l in range(n_layers):
        lam_init = _lambda_init(l)
        scale = hd ** -0.5 * math.log2(math.e)
        w_qk = w_in[l, :, :2 * aw].reshape(D, 2 * aw // hd, hd)[:, :, perm].reshape(D, 2 * aw)
        w_in_bf = jnp.concatenate([w_qk, w_in[l, :, 2 * aw:]], axis=1).astype(BF16)
        head_gain = jnp.concatenate([jnp.tile(q_norm[l][perm] * scale, aw // hd),
                                     jnp.tile(k_norm[l][perm], aw // hd)]).reshape(1, 2 * aw)
        lamv = jnp.stack([lambda_q1[l], lambda_k1[l], lambda_q2[l], lambda_k2[l]]).astype(F32)
        w_pool_bf = w_pool[l].astype(BF16)
        wa_bf = w_out[l, :aw].astype(BF16)
        wp_bf = w_out[l, aw:].astype(BF16)
        w_r = jnp.concatenate([w_router_group[l], w_router_expert[l]], axis=1)
        w_r = jnp.pad(w_r, ((0, 0), (0, LANES - w_r.shape[1])))
        w_r_hi = w_r.astype(BF16)
        w_r_lo = (w_r - w_r_hi.astype(F32)).astype(BF16)
        b_r = jnp.pad(jnp.concatenate([b_router_group[l], b_router_expert[l]]).astype(F32),
                      (0, LANES - n_groups - n_experts)).reshape(1, LANES)
        dff = w_gate.shape[-1]
        inproj_side = [w_gate[l].reshape(n_experts * D, dff), w_up[l].reshape(n_experts * D, dff)]
        outproj_side = [w_down[l].reshape(n_experts * dff, D), None]
        expert_w = []

        x1s = []
        for tr, side_in, side_out in zip(trunks, inproj_side, outproj_side):
            proj, side_bf = _inproj(tr["x"], norm_mix[l].reshape(1, D), w_in_bf, head_gain, cos_t, sx_t,
                                    side_in, seq=tr["seq"], aw=aw, hd=hd)
            expert_w.append(side_bf.reshape(n_experts, D, dff))
            a = _attention(proj, lamv, subln[l].reshape(1, 2 * hd), batch=tr["batch"], seq=tr["seq"],
                           aw=aw, hd=hd, lam_init=lam_init)
            p = _pool(proj, w_pool_bf, pool_scale[l].reshape(1, pw), tr["bands"], batch=tr["batch"],
                      seq=tr["seq"], aw=aw, pw=pw)
            if side_out is None:
                x1s.append(_outproj(a, p, wa_bf, wp_bf, tr["x"]))
            else:
                x1, side_bf = _outproj(a, p, wa_bf, wp_bf, tr["x"], side_out)
                x1s.append(x1)
                expert_w.append(side_bf.reshape(n_experts, dff, D))
        wg_bf, wd_bf, wu_bf = expert_w

        xn, eid, gate = _router(x1s[0], x1s[1], norm_ffn[l].reshape(1, D), w_r_hi, w_r_lo, b_r,
                                n_groups=n_groups, n_experts=n_experts)
        te, tlen, n_tiles, tok_tbl, pos, _ = _moe_plan(eid[:, :TOP_K], tm=tm_moe, n_experts=n_experts)
        ys = _experts(te, n_tiles, tlen, tok_tbl, xn, wg_bf, wu_bf, wd_bf, tm=tm_moe)
        t_all = t_a + t_b
        pos_tbl = pos.reshape(t_all // tt_cmb, tt_cmb, TOP_K).transpose(0, 2, 1).reshape(
            t_all // tt_cmb, TOP_K * tt_cmb)
        outs = []
        tile_off = 0
        for tr, x1 in zip(trunks, x1s):
            outs.append(_combine(pos_tbl, ys, x1, gate, tile_off=tile_off, tt=tt_cmb))
            tile_off += x1.shape[0] // tt_cmb
        for tr, o in zip(trunks, outs):
            tr["x"] = o

    return tuple(tr["x"].reshape(tr["batch"], tr["seq"], D) for tr in trunks)
```

```python
import functools
import math

import jax
import jax.numpy as jnp
import numpy as np
from jax import lax
from jax.experimental import pallas as pl
from jax.experimental.pallas import tpu as pltpu

EPS = 1e-6
ROPE_THETA = 500000.0
POOL_WINDOWS = (2, 4, 8, 16)
TOP_K = 2
POOL_HALO = 16
LANES = 128
VMEM_CAP_BYTES = 60000 * 1024
F32 = jnp.float32
BF16 = jnp.bfloat16


def _lambda_init(layer_idx):
    return 0.8 - 0.6 * math.exp(-0.3 * layer_idx)


def _vmem_limit(est_bytes):
    return int(min(VMEM_CAP_BYTES, est_bytes * 5 // 4 + (4 << 20)))


def _params(sem, est_bytes):
    return pltpu.CompilerParams(dimension_semantics=sem, vmem_limit_bytes=_vmem_limit(est_bytes))


def _cast_block_rows(rows, steps):
    rb = -(-rows // steps)
    while rows % rb or rb % 16:
        rb += 1
    return rb


def _inproj_kernel(x_ref, g_ref, w_ref, hg_ref, cos_ref, sx_ref, wc_ref, o_ref, wc_out_ref,
                   xn_ref, acc0_ref, acc1_ref, *, n_qk_tiles, n_col_tiles, hd):
    j = pl.program_id(1)
    tn = w_ref.shape[1]
    accs = (acc0_ref, acc1_ref)

    def cast_side():
        wc_out_ref[...] = wc_ref[...].astype(wc_out_ref.dtype)

    @pl.when(j == 0)
    def _():
        x = x_ref[...]
        ms = jnp.mean(x * x, axis=-1, keepdims=True)
        xn_ref[...] = (x * lax.rsqrt(ms + EPS) * g_ref[...]).astype(xn_ref.dtype)

    def matmul_into(acc_ref):
        acc_ref[...] = jnp.dot(xn_ref[...], w_ref[...], preferred_element_type=F32)

    def qk_epilogue(acc_ref):
        cos = cos_ref[...]
        sx = sx_ref[...]
        for c in range(tn // hd):
            blk = acc_ref[:, c * hd:(c + 1) * hd]
            ms = jnp.mean(blk * blk, axis=-1, keepdims=True)
            y = blk * lax.rsqrt(ms + EPS) * hg_ref[:, c * hd:(c + 1) * hd]
            y = y * cos + pltpu.roll(y, hd // 2, 1) * sx
            o_ref[:, c * hd:(c + 1) * hd] = y.astype(o_ref.dtype)

    def plain_epilogue(acc_ref):
        o_ref[...] = acc_ref[...].astype(o_ref.dtype)

    @pl.when(j == 0)
    def _():
        cast_side()
        matmul_into(accs[0])

    for parity in range(2):
        cur, prev = accs[parity], accs[1 - parity]
        is_par = (j % 2) == parity

        @pl.when(jnp.logical_and(is_par, jnp.logical_and(j >= 1, j <= n_qk_tiles)))
        def _(cur=cur, prev=prev):
            cast_side()
            matmul_into(cur)
            qk_epilogue(prev)

        @pl.when(jnp.logical_and(is_par, jnp.logical_and(j > n_qk_tiles, j < n_col_tiles)))
        def _(cur=cur, prev=prev):
            cast_side()
            matmul_into(cur)
            plain_epilogue(prev)

    @pl.when(j == n_col_tiles)
    def _():
        cast_side()
        plain_epilogue(accs[(n_col_tiles - 1) % 2])


def _inproj(x2, gain, w_bf, head_gain, cos_t, sx_t, side_w, *, seq, aw, hd):
    T, D = x2.shape
    N = w_bf.shape[1]
    tm = min(512, seq)
    tn = min(1024, 2 * aw)
    n_qk_tiles = (2 * aw) // tn
    n_col_tiles = N // tn
    assert n_qk_tiles < n_col_tiles
    s_tiles = seq // tm
    n_steps = n_col_tiles + 1
    rows, cols = side_w.shape
    rb = _cast_block_rows(rows, (T // tm) * n_steps)
    cast_blocks = rows // rb
    cast_ix = lambda i, j: (jnp.minimum(i * n_steps + j, cast_blocks - 1), 0)
    est = (2 * tm * D * 4 + 2 * D * tn * 2 + tm * D * 2 + 2 * tm * tn * 2 + 4 * tm * tn * 4
           + 2 * rb * cols * (4 + 2))
    kern = functools.partial(_inproj_kernel, n_qk_tiles=n_qk_tiles, n_col_tiles=n_col_tiles, hd=hd)
    return pl.pallas_call(
        kern,
        grid=(T // tm, n_steps),
        in_specs=[
            pl.BlockSpec((tm, D), lambda i, j: (i, 0)),
            pl.BlockSpec((1, D), lambda i, j: (0, 0)),
            pl.BlockSpec((D, tn), lambda i, j: (0, jnp.minimum(j, n_col_tiles - 1))),
            pl.BlockSpec((1, tn), lambda i, j: (0, jnp.clip(j - 1, 0, n_qk_tiles - 1))),
            pl.BlockSpec((tm, hd), lambda i, j: (i % s_tiles, 0)),
            pl.BlockSpec((tm, hd), lambda i, j: (i % s_tiles, 0)),
            pl.BlockSpec((rb, cols), cast_ix),
        ],
        out_specs=[
            pl.BlockSpec((tm, tn), lambda i, j: (i, jnp.maximum(j - 1, 0))),
            pl.BlockSpec((rb, cols), cast_ix),
        ],
        out_shape=[jax.ShapeDtypeStruct((T, N), BF16), jax.ShapeDtypeStruct((rows, cols), BF16)],
        scratch_shapes=[pltpu.VMEM((tm, D), BF16), pltpu.VMEM((tm, tn), F32), pltpu.VMEM((tm, tn), F32)],
        compiler_params=_params(("arbitrary", "arbitrary"), est),
        name="inproj",
    )(x2, gain, w_bf, head_gain, cos_t, sx_t, side_w)


def _attn_kernel(lam_ref, subln_ref, q_ref, k_ref, v_ref, o_ref,
                 vt_ref, e_ref, mc_ref, lc_ref, acc_ref, *, tk, sub, hd, lam_init, unroll):
    seq = k_ref.shape[0]
    nck = seq // tk
    contract_last = (((1,), (1,)), ((), ()))

    @pl.when(pl.program_id(2) == 0)
    def _():
        def transpose_v(c, carry):
            vc = v_ref[pl.ds(pl.multiple_of(c * tk, tk), tk), :]
            vt_ref[c] = vc.astype(F32).T.astype(vt_ref.dtype)
            return carry

        lax.fori_loop(0, nck, transpose_v, 0)

    q = q_ref[...]
    tq = q.shape[0]
    nsub = tk // sub
    sl = 8

    def pass1(c, carry):
        kc = k_ref[pl.ds(pl.multiple_of(c * tk, tk), tk), :]
        for comp in range(2):
            st = lax.dot_general(kc[:, comp * hd:(comp + 1) * hd], q[:, comp * hd:(comp + 1) * hd],
                                 contract_last, preferred_element_type=F32)
            for j in range(nsub):
                sj = st[j * sub:(j + 1) * sub].reshape(sub // sl, sl, tq)
                mj = jnp.max(sj, axis=0)
                ej = jnp.exp2(sj - mj[None])
                mc_ref[comp, c, j] = mj
                lc_ref[comp, c, j] = jnp.sum(ej, axis=0)
                e_ref[comp, c, j * sub:(j + 1) * sub, :] = ej.reshape(sub, tq).astype(e_ref.dtype)
        return carry

    lax.fori_loop(0, nck, pass1, 0, unroll=unroll)

    lv = lam_ref[...]
    lam = (jnp.exp(jnp.sum(lv[0:1] * lv[1:2], axis=1, keepdims=True))
           - jnp.exp(jnp.sum(lv[2:3] * lv[3:4], axis=1, keepdims=True)) + lam_init)
    def over_blocks(red, x):
        return red(red(red(x, axis=0), axis=0), axis=0, keepdims=True)

    m0 = over_blocks(jnp.max, mc_ref[0])
    m1 = over_blocks(jnp.max, mc_ref[1])
    r0 = 1.0 / over_blocks(jnp.sum, lc_ref[0] * jnp.exp2(mc_ref[0] - m0))
    r1 = lam / over_blocks(jnp.sum, lc_ref[1] * jnp.exp2(mc_ref[1] - m1))
    acc_ref[...] = jnp.zeros(acc_ref.shape, F32)

    def pass2(c, carry):
        f0 = jnp.exp2(mc_ref[0, c] - m0) * r0
        f1 = jnp.exp2(mc_ref[1, c] - m1) * r1
        pk = 2 * sl
        blocks = []
        for j in range(nsub):
            g0 = jnp.concatenate([f0[j], f0[j]], axis=0).astype(e_ref.dtype)
            g1 = jnp.concatenate([f1[j], f1[j]], axis=0).astype(e_ref.dtype)
            e0 = e_ref[0, c, j * sub:(j + 1) * sub, :].reshape(sub // pk, pk, tq)
            e1 = e_ref[1, c, j * sub:(j + 1) * sub, :].reshape(sub // pk, pk, tq)
            blocks.append((e0 * g0[None] - e1 * g1[None]).reshape(sub, tq))
        wt = jnp.concatenate(blocks, axis=0)
        acc_ref[...] += jnp.dot(vt_ref[c], wt, preferred_element_type=F32)
        return carry

    lax.fori_loop(0, nck, pass2, 0, unroll=unroll)

    o = acc_ref[...].T
    ms = jnp.mean(o * o, axis=-1, keepdims=True)
    o = o * lax.rsqrt(ms + EPS) * subln_ref[...]
    o_ref[...] = (o * (1.0 - lam_init)).astype(o_ref.dtype)


def _attention(proj, lamv, subln, *, batch, seq, aw, hd, lam_init):
    T = proj.shape[0]
    vd = 2 * hd
    n_heads = aw // vd
    tq = min(seq, 512)
    tk = min(seq, 512)
    nck = seq // tk
    q_tiles = seq // tq
    k_col0 = aw // vd
    v_col0 = 2 * aw // vd
    sub = min(tk, 128)
    nsub = tk // sub
    est = (2 * 2 * seq * vd * 2 + seq * vd * 2 + 2 * seq * tq * 2 + 2 * 2 * nck * nsub * 8 * tq * 4
           + tq * vd * 4 + 4 * tq * vd * 2 + 6 * tq * tk * 4)
    kern = functools.partial(_attn_kernel, tk=tk, sub=sub, hd=hd, lam_init=lam_init, unroll=min(8, nck))
    return pl.pallas_call(
        kern,
        grid=(batch, n_heads, q_tiles),
        in_specs=[
            pl.BlockSpec((4, hd), lambda b, h, i: (0, 0)),
            pl.BlockSpec((1, vd), lambda b, h, i: (0, 0)),
            pl.BlockSpec((tq, vd), lambda b, h, i: (b * q_tiles + i, h)),
            pl.BlockSpec((seq, vd), lambda b, h, i: (b, k_col0 + h)),
            pl.BlockSpec((seq, vd), lambda b, h, i: (b, v_col0 + h)),
        ],
        out_specs=pl.BlockSpec((tq, vd), lambda b, h, i: (b * q_tiles + i, h)),
        out_shape=jax.ShapeDtypeStruct((T, aw), BF16),
        scratch_shapes=[
            pltpu.VMEM((nck, vd, tk), BF16),
            pltpu.VMEM((2, nck, tk, tq), BF16),
            pltpu.VMEM((2, nck, nsub, 8, tq), F32),
            pltpu.VMEM((2, nck, nsub, 8, tq), F32),
            pltpu.VMEM((vd, tq), F32),
        ],
        compiler_params=_params(("arbitrary", "arbitrary", "arbitrary"), est),
        name="diff_attention",
    )(lamv, subln, proj, proj, proj)


def _pool_bands(rows):
    t = np.arange(rows)[:, None]
    jm = np.arange(rows)[None, :]
    jh = np.arange(POOL_HALO)[None, :]
    bm, bp, bn = [], [], []
    for w in POOL_WINDOWS:
        left = w // 2
        right = w - 1 - left
        bm.append((jm >= t - left) & (jm <= t + right))
        bp.append(jh - POOL_HALO >= t - left)
        bn.append(rows + jh <= t + right)
    as_bf = lambda m: jnp.asarray(np.stack(m).astype(np.float32), dtype=BF16)
    return as_bf(bm), as_bf(bp), as_bf(bn)


def _pool_kernel(u_ref, wp_ref, ps_ref, bm_ref, bp_ref, bn_ref, o_ref, *, rows):
    g = pl.program_id(1)
    seq = u_ref.shape[0]
    n_chunks = seq // rows
    for gi, w in enumerate(POOL_WINDOWS):
        left = w // 2
        right = w - 1 - left

        @pl.when(g == gi)
        def _(gi=gi, left=left, right=right):
            def body(r, carry):
                r0 = pl.multiple_of(r * rows, rows)
                main = u_ref[pl.ds(r0, rows), :]
                p0 = pl.multiple_of(jnp.maximum(r0 - POOL_HALO, 0), POOL_HALO)
                n0 = pl.multiple_of(jnp.minimum(r0 + rows, seq - POOL_HALO), POOL_HALO)
                prev = u_ref[pl.ds(p0, POOL_HALO), :]
                nxt = u_ref[pl.ds(n0, POOL_HALO), :]
                prev = jnp.where(r > 0, prev, jnp.zeros_like(prev))
                nxt = jnp.where(r < n_chunks - 1, nxt, jnp.zeros_like(nxt))
                ssum = (jnp.dot(bm_ref[gi], main, preferred_element_type=F32)
                        + jnp.dot(bp_ref[gi], prev, preferred_element_type=F32)
                        + jnp.dot(bn_ref[gi], nxt, preferred_element_type=F32))
                t = r0 + lax.broadcasted_iota(jnp.int32, (rows, 1), 0)
                lo = jnp.maximum(t - left, 0)
                hi = jnp.minimum(t + right, seq - 1)
                cnt = (hi - lo + 1).astype(F32)
                delta = ssum / cnt - main.astype(F32)
                y = jnp.dot(delta.astype(wp_ref.dtype), wp_ref[...], preferred_element_type=F32)
                o_ref[pl.ds(r0, rows), :] = (y * ps_ref[...]).astype(o_ref.dtype)
                return carry

            lax.fori_loop(0, n_chunks, body, 0)


def _pool(proj, w_pool_bf, pool_scale, bands, *, batch, seq, aw, pw):
    T = proj.shape[0]
    n_groups, gw, _ = w_pool_bf.shape
    rows = min(seq, 512)
    u_col0 = 3 * aw // gw
    bm, bp, bn = bands
    est = (2 * 2 * seq * gw * 2 + 2 * gw * gw * 2 + 2 * 4 * rows * (rows + 2 * POOL_HALO) * 2
           + 8 * rows * gw * 4)
    kern = functools.partial(_pool_kernel, rows=rows)
    return pl.pallas_call(
        kern,
        grid=(batch, n_groups),
        in_specs=[
            pl.BlockSpec((seq, gw), lambda b, g: (b, u_col0 + g)),
            pl.BlockSpec((None, gw, gw), lambda b, g: (g, 0, 0)),
            pl.BlockSpec((1, gw), lambda b, g: (0, g)),
            pl.BlockSpec(bm.shape, lambda b, g: (0, 0, 0)),
            pl.BlockSpec(bp.shape, lambda b, g: (0, 0, 0)),
            pl.BlockSpec(bn.shape, lambda b, g: (0, 0, 0)),
        ],
        out_specs=pl.BlockSpec((seq, gw), lambda b, g: (b, g)),
        out_shape=jax.ShapeDtypeStruct((T, pw), BF16),
        compiler_params=_params(("arbitrary", "arbitrary"), est),
        name="multiscale_pool",
    )(proj, w_pool_bf, pool_scale, bm, bp, bn)


def _outproj_kernel(a_ref, p_ref, wa_ref, wp_ref, x_ref, *rest):
    if len(rest) == 3:
        wc_ref, o_ref, wc_out_ref = rest
        wc_out_ref[...] = wc_ref[...].astype(wc_out_ref.dtype)
    else:
        (o_ref,) = rest
    acc = (jnp.dot(a_ref[...], wa_ref[...], preferred_element_type=F32)
           + jnp.dot(p_ref[...], wp_ref[...], preferred_element_type=F32))
    o_ref[...] = x_ref[...] + acc


def _outproj(a, p, wa_bf, wp_bf, x2, side_w=None):
    T, D = x2.shape
    aw = a.shape[1]
    pw = p.shape[1]
    tm = min(512, T)
    tn = min(1024, D)
    n_col = D // tn
    est = 2 * (tm * aw * 2 + tm * pw * 2 + (aw + pw) * tn * 2 + 2 * tm * tn * 4) + 2 * tm * tn * 4
    in_specs = [
        pl.BlockSpec((tm, aw), lambda i, j: (i, 0)),
        pl.BlockSpec((tm, pw), lambda i, j: (i, 0)),
        pl.BlockSpec((aw, tn), lambda i, j: (0, j)),
        pl.BlockSpec((pw, tn), lambda i, j: (0, j)),
        pl.BlockSpec((tm, tn), lambda i, j: (i, j)),
    ]
    out_specs = [pl.BlockSpec((tm, tn), lambda i, j: (i, j))]
    out_shape = [jax.ShapeDtypeStruct((T, D), F32)]
    args = [a, p, wa_bf, wp_bf, x2]
    if side_w is not None:
        rows, cols = side_w.shape
        rb = _cast_block_rows(rows, (T // tm) * n_col)
        cast_blocks = rows // rb
        cast_ix = lambda i, j: (jnp.minimum(i * n_col + j, cast_blocks - 1), 0)
        in_specs.append(pl.BlockSpec((rb, cols), cast_ix))
        out_specs.append(pl.BlockSpec((rb, cols), cast_ix))
        out_shape.append(jax.ShapeDtypeStruct((rows, cols), BF16))
        args.append(side_w)
        est += 2 * rb * cols * (4 + 2)
    res = pl.pallas_call(
        _outproj_kernel,
        grid=(T // tm, n_col),
        in_specs=in_specs,
        out_specs=out_specs,
        out_shape=out_shape,
        compiler_params=_params(("arbitrary", "arbitrary"), est),
        name="outproj",
    )(*args)
    return res if side_w is not None else res[0]


def _router_kernel(xa_ref, xb_ref, g_ref, whi_ref, wlo_ref, b_ref, xn_ref, eid_ref, gate_ref, *,
                   a_tiles, n_groups, n_experts):
    i = pl.program_id(0)
    epg_shift = int(math.log2(n_experts // n_groups))

    def run(x_ref):
        x = x_ref[...]
        ms = jnp.mean(x * x, axis=-1, keepdims=True)
        xn = x * lax.rsqrt(ms + EPS) * g_ref[...]
        hi = xn.astype(BF16)
        xn_ref[...] = hi.astype(F32)
        lo = (xn - hi.astype(F32)).astype(BF16)
        whi = whi_ref[...]
        logits = (jnp.dot(hi, whi, preferred_element_type=F32)
                  + jnp.dot(lo, whi, preferred_element_type=F32)
                  + jnp.dot(hi, wlo_ref[...], preferred_element_type=F32)
                  + b_ref[...])
        lane = lax.broadcasted_iota(jnp.int32, logits.shape, 1)
        neg = jnp.float32(-jnp.inf)
        is_g = lane < n_groups
        gl = jnp.where(is_g, logits, neg)
        gmax = jnp.max(gl, axis=1, keepdims=True)
        g_idx = jnp.min(jnp.where(gl == gmax, lane, LANES), axis=1, keepdims=True)
        gsum = jnp.sum(jnp.where(is_g, jnp.exp(gl - gmax), 0.0), axis=1, keepdims=True)
        g_w = 1.0 / gsum
        e_lane = lane - n_groups
        grp = jnp.where(jnp.logical_and(e_lane >= 0, e_lane < n_experts),
                        lax.shift_right_arithmetic(e_lane, epg_shift), -1)
        sel = grp == g_idx
        el = jnp.where(sel, logits, neg)
        emax = jnp.max(el, axis=1, keepdims=True)
        ex = jnp.where(sel, jnp.exp(el - emax), 0.0)
        prob = ex / jnp.sum(ex, axis=1, keepdims=True)
        pm = jnp.where(sel, prob, -1.0)
        p1 = jnp.max(pm, axis=1, keepdims=True)
        i1 = jnp.min(jnp.where(pm == p1, lane, LANES), axis=1, keepdims=True)
        pm2 = jnp.where(lane == i1, -1.0, pm)
        p2 = jnp.max(pm2, axis=1, keepdims=True)
        i2 = jnp.min(jnp.where(pm2 == p2, lane, LANES), axis=1, keepdims=True)
        tsum = p1 + p2
        gate1 = g_w * (p1 / tsum)
        gate2 = g_w * (p2 / tsum)
        eid_ref[...] = jnp.where(lane == 0, i1 - n_groups, jnp.where(lane == 1, i2 - n_groups, 0))
        gate_ref[...] = jnp.where(lane == 0, gate1, jnp.where(lane == 1, gate2, 0.0))

    @pl.when(i < a_tiles)
    def _():
        run(xa_ref)

    @pl.when(i >= a_tiles)
    def _():
        run(xb_ref)


def _router(x1a, x1b, gain, whi, wlo, bias, *, n_groups, n_experts):
    Ta, D = x1a.shape
    Tb = x1b.shape[0]
    tt = min(256, Ta, Tb)
    a_tiles = Ta // tt
    b_tiles = Tb // tt
    T = Ta + Tb
    est = 2 * 2 * tt * D * 4 + 2 * tt * D * 4 + 4 * D * LANES * 2 + 6 * tt * D * 4
    kern = functools.partial(_router_kernel, a_tiles=a_tiles, n_groups=n_groups, n_experts=n_experts)
    return pl.pallas_call(
        kern,
        grid=(a_tiles + b_tiles,),
        in_specs=[
            pl.BlockSpec((tt, D), lambda i: (jnp.minimum(i, a_tiles - 1), 0)),
            pl.BlockSpec((tt, D), lambda i: (jnp.maximum(i - a_tiles, 0), 0)),
            pl.BlockSpec((1, D), lambda i: (0, 0)),
            pl.BlockSpec((D, LANES), lambda i: (0, 0)),
            pl.BlockSpec((D, LANES), lambda i: (0, 0)),
            pl.BlockSpec((1, LANES), lambda i: (0, 0)),
        ],
        out_specs=[
            pl.BlockSpec((tt, D), lambda i: (i, 0)),
            pl.BlockSpec((tt, LANES), lambda i: (i, 0)),
            pl.BlockSpec((tt, LANES), lambda i: (i, 0)),
        ],
        out_shape=[
            jax.ShapeDtypeStruct((T, D), F32),
            jax.ShapeDtypeStruct((T, LANES), jnp.int32),
            jax.ShapeDtypeStruct((T, LANES), F32),
        ],
        compiler_params=_params(("arbitrary",), est),
        name="router",
    )(x1a, x1b, gain, whi, wlo, bias)


def _moe_plan(eid2, *, tm, n_experts):
    T = eid2.shape[0]
    A = T * TOP_K
    eflat = eid2.reshape(A)
    experts = jnp.arange(n_experts, dtype=jnp.int32)
    sorted_e, order = lax.sort_key_val(eflat, jnp.arange(A, dtype=jnp.int32))
    counts = jnp.sum((eflat[None, :] == experts[:, None]).astype(jnp.int32), axis=1)
    ends = jnp.cumsum(counts).astype(jnp.int32)
    starts = ends - counts
    ntile_e = (counts + tm - 1) // tm
    tile_end = jnp.cumsum(ntile_e).astype(jnp.int32)
    tile_base = tile_end - ntile_e
    n_tiles = tile_end[-1]
    G = -(-A // tm) + n_experts
    ti = jnp.arange(G, dtype=jnp.int32)
    valid = ti < n_tiles
    te = jnp.sum((ti[:, None] >= tile_end[None, :]).astype(jnp.int32), axis=1)
    te = jnp.minimum(te, n_experts - 1)
    te = jnp.where(valid, te, te[n_tiles - 1])
    j = ti - tile_base[te]
    tstart = starts[te] + j * tm
    tlen = jnp.where(valid, jnp.clip(counts[te] - j * tm, 0, tm), 0).astype(jnp.int32)
    r = jnp.arange(tm, dtype=jnp.int32)
    sidx = jnp.clip(tstart[:, None] + r[None, :], 0, A - 1)
    tok_tbl = jnp.where(r[None, :] < tlen[:, None], order[sidx] // TOP_K, 0).astype(jnp.int32)
    rank = jnp.arange(A, dtype=jnp.int32) - starts[sorted_e]
    row = tile_base[sorted_e] * tm + rank
    _, pos = lax.sort_key_val(order, row)
    pos = pos.reshape(T, TOP_K)
    return te, tlen, n_tiles.reshape(1), tok_tbl, pos, G


def _expert_kernel(te_ref, nt_ref, tl_ref, tbl_ref, xn_ref, wg_ref, wu_ref, wd_ref, o_ref,
                   idx_ref, xrow_ref, xb_ref, h_ref, sem_idx, sem_rows, *,
                   n_tiles, fa, tf, half, grp):
    del te_ref, nt_ref
    i = pl.program_id(0)
    s = pl.program_id(1)
    tm = xrow_ref.shape[0]
    n_grp = tm // grp
    ln = tl_ref[i]

    def idx_copy(tile):
        dst = idx_ref.at[pl.ds(pl.multiple_of((tile % 2) * tm, tm), tm)]
        return pltpu.make_async_copy(tbl_ref.at[tile], dst, sem_idx.at[tile % 2])

    def issue_rows(tile):
        cnt = tl_ref[tile]
        base = (tile % 2) * tm
        for g in range(n_grp):
            @pl.when(g * grp < cnt)
            def _(g=g):
                for r in range(g * grp, (g + 1) * grp):
                    tok = idx_ref[base + r]
                    pltpu.make_async_copy(xn_ref.at[tok], xrow_ref.at[r], sem_rows).start()

    def wait_rows():
        for g in range(n_grp):
            @pl.when(g * grp < ln)
            def _(g=g):
                blk = xrow_ref.at[pl.ds(g * grp, grp)]
                pltpu.make_async_copy(blk, blk, sem_rows).wait()

    @pl.when(jnp.logical_and(i == 0, s == 0))
    def _():
        xrow_ref[...] = jnp.zeros(xrow_ref.shape, xrow_ref.dtype)
        first = idx_copy(0)
        first.start()
        first.wait()
        n_first = ((tl_ref[0] + grp - 1) // grp) * grp

        def issue_first(r, carry):
            pltpu.make_async_copy(xn_ref.at[idx_ref[r]], xrow_ref.at[r], sem_rows).start()
            return carry

        lax.fori_loop(0, n_first, issue_first, 0)
        if n_tiles > 1:
            idx_copy(1).start()

    @pl.when(s == 0)
    def _():
        wait_rows()
        xb_ref[...] = xrow_ref[...].astype(xb_ref.dtype)

    @pl.when(jnp.logical_and(s == 1, i + 1 < n_tiles))
    def _():
        idx_copy(i + 1).wait()
        issue_rows(i + 1)

        @pl.when(i + 2 < n_tiles)
        def _():
            idx_copy(i + 2).start()

    def rows_variants(fn):
        @pl.when(ln > half)
        def _():
            fn(tm)

        @pl.when(jnp.logical_and(ln > 0, ln <= half))
        def _():
            fn(half)

    for f in range(fa):
        @pl.when(s == f)
        def _(f=f):
            def gate_up(nrows):
                x = xb_ref[0:nrows, :]
                gate = jnp.dot(x, wg_ref[...], preferred_element_type=F32)
                up = jnp.dot(x, wu_ref[...], preferred_element_type=F32)
                h_ref[0:nrows, f * tf:(f + 1) * tf] = (jax.nn.silu(gate) * up).astype(h_ref.dtype)

            rows_variants(gate_up)

    @pl.when(s >= fa)
    def _():
        def down(nrows):
            o_ref[0:nrows, :] = jnp.dot(h_ref[0:nrows, :], wd_ref[...], preferred_element_type=F32)

        rows_variants(down)

        @pl.when(ln <= half)
        def _():
            o_ref[half:tm, :] = jnp.zeros((tm - half, o_ref.shape[1]), o_ref.dtype)

        @pl.when(ln == 0)
        def _():
            o_ref[0:half, :] = jnp.zeros((half, o_ref.shape[1]), o_ref.dtype)


def _experts(te, n_tiles, tlen, tok_tbl, xn, wg_bf, wu_bf, wd_bf, *, tm):
    G = te.shape[0]
    E, D, dff = wg_bf.shape
    tf = min(256, dff)
    fa = dff // tf
    tn = min(2048, D)
    fb = D // tn
    half = tm // 2
    grp = min(64, tm)
    assert fa >= 2, "the next tile's gather is issued at inner step 1"
    est = (tm * D * 4 + tm * D * 2 + tm * dff * 2 + 2 * 2 * D * tf * 2 + 2 * dff * tn * 2
           + 2 * tm * tn * 4 + 3 * tm * tf * 4 + tm * tn * 4)

    def a_ix(i, s, nt):
        return jnp.where(i < nt[0], jnp.minimum(s, fa - 1), fa - 1)

    def b_ix(s):
        return jnp.clip(s - fa, 0, fb - 1)

    def wd_ix(i, s, nt):
        return jnp.where(i < nt[0], b_ix(s), fb - 1)

    kern = functools.partial(_expert_kernel, n_tiles=G, fa=fa, tf=tf, half=half, grp=grp)
    return pl.pallas_call(
        kern,
        grid_spec=pltpu.PrefetchScalarGridSpec(
            num_scalar_prefetch=3,
            grid=(G, fa + fb),
            in_specs=[
                pl.BlockSpec(memory_space=pl.ANY),
                pl.BlockSpec(memory_space=pl.ANY),
                pl.BlockSpec((None, D, tf), lambda i, s, te, nt, tl: (te[i], 0, a_ix(i, s, nt))),
                pl.BlockSpec((None, D, tf), lambda i, s, te, nt, tl: (te[i], 0, a_ix(i, s, nt))),
                pl.BlockSpec((None, dff, tn), lambda i, s, te, nt, tl: (te[i], 0, wd_ix(i, s, nt))),
            ],
            out_specs=pl.BlockSpec((tm, tn), lambda i, s, te, nt, tl: (i, b_ix(s))),
            scratch_shapes=[
                pltpu.SMEM((2 * tm,), jnp.int32),
                pltpu.VMEM((tm, D), F32),
                pltpu.VMEM((tm, D), BF16),
                pltpu.VMEM((tm, dff), BF16),
                pltpu.SemaphoreType.DMA((2,)),
                pltpu.SemaphoreType.DMA(()),
            ],
        ),
        out_shape=jax.ShapeDtypeStruct((G * tm, D), F32),
        compiler_params=_params(("arbitrary", "arbitrary"), est),
        name="moe_experts",
    )(te, n_tiles, tlen, tok_tbl, xn, wg_bf, wu_bf, wd_bf)


def _combine_kernel(pos_ref, ys_ref, x1_ref, gate_ref, o_ref, idx_ref, ybuf_ref, sem_idx, sem_rows, *,
                    tile_off, n_tiles, tt):
    i = pl.program_id(0)
    slot = i % 2

    def idx_copy(tile, s):
        return pltpu.make_async_copy(pos_ref.at[tile + tile_off], idx_ref.at[s], sem_idx.at[s])

    def issue_rows(s):
        def body(r, carry):
            row = idx_ref[s, r]
            pltpu.make_async_copy(ys_ref.at[row], ybuf_ref.at[s, r], sem_rows.at[s]).start()
            return carry

        lax.fori_loop(0, TOP_K * tt, body, 0, unroll=8)

    @pl.when(i == 0)
    def _():
        first = idx_copy(0, 0)
        first.start()
        first.wait()
        issue_rows(0)
        if n_tiles > 1:
            idx_copy(1, 1).start()

    @pl.when(i + 1 < n_tiles)
    def _():
        idx_copy(i + 1, 1 - slot).wait()
        issue_rows(1 - slot)

        @pl.when(i + 2 < n_tiles)
        def _():
            idx_copy(i + 2, slot).start()

    buf = ybuf_ref.at[slot]
    pltpu.make_async_copy(buf, buf, sem_rows.at[slot]).wait()
    g = gate_ref[...]
    y0 = ybuf_ref[slot, 0:tt, :]
    y1 = ybuf_ref[slot, tt:2 * tt, :]
    o_ref[...] = x1_ref[...] + (y0 * g[:, 0:1] + y1 * g[:, 1:2])


def _combine(pos_tbl, ys, x1, gate, *, tile_off, tt):
    T, D = x1.shape
    n_tiles = T // tt
    est = 2 * TOP_K * tt * D * 4 + 2 * 2 * tt * D * 4 + 2 * tt * LANES * 4 + 3 * tt * D * 4
    kern = functools.partial(_combine_kernel, tile_off=tile_off, n_tiles=n_tiles, tt=tt)
    return pl.pallas_call(
        kern,
        grid=(n_tiles,),
        in_specs=[
            pl.BlockSpec(memory_space=pl.ANY),
            pl.BlockSpec(memory_space=pl.ANY),
            pl.BlockSpec((tt, D), lambda i: (i, 0)),
            pl.BlockSpec((tt, LANES), lambda i: (i + tile_off, 0)),
        ],
        out_specs=pl.BlockSpec((tt, D), lambda i: (i, 0)),
        out_shape=jax.ShapeDtypeStruct((T, D), F32),
        scratch_shapes=[
            pltpu.SMEM((2, TOP_K * tt), jnp.int32),
            pltpu.VMEM((2, TOP_K * tt, D), F32),
            pltpu.SemaphoreType.DMA((2,)),
            pltpu.SemaphoreType.DMA((2,)),
        ],
        compiler_params=_params(("arbitrary",), est),
        name="moe_combine",
    )(pos_tbl, ys, x1, gate)


def _rope_perm(hd, rot_dim):
    half = rot_dim // 2
    mid = hd // 2
    assert rot_dim <= mid
    perm = np.arange(hd)
    perm[half:rot_dim] = np.arange(mid, mid + half)
    perm[mid:mid + half] = np.arange(half, rot_dim)
    return perm


def _rope_tables(seq, hd, rot_dim):
    half = rot_dim // 2
    mid = hd // 2
    inv_freq = ROPE_THETA ** (-jnp.arange(half, dtype=F32) * 2.0 / rot_dim)
    ang = jnp.arange(seq, dtype=F32)[:, None] * inv_freq[None, :]
    cos, sin = jnp.cos(ang), jnp.sin(ang)
    ones = lambda n: jnp.ones((seq, n), F32)
    zeros = lambda n: jnp.zeros((seq, n), F32)
    cos_t = jnp.concatenate([cos, ones(mid - half), cos, ones(hd - mid - half)], axis=1)
    sx_t = jnp.concatenate([-sin, zeros(mid - half), sin, zeros(hd - mid - half)], axis=1)
    return cos_t, sx_t


def kernel(x_prompt, x_sample, norm_mix, w_in, q_norm, k_norm, lambda_q1, lambda_k1, lambda_q2,
           lambda_k2, subln, w_pool, pool_scale, w_out, norm_ffn, w_router_group, b_router_group,
           w_router_expert, b_router_expert, w_gate, w_up, w_down):
    n_layers, D, in_cols = w_in.shape
    mix = w_out.shape[1]
    aw = (in_cols - mix) // 2
    pw = mix - aw
    hd = q_norm.shape[-1]
    rot_dim = hd // 4
    n_groups = w_router_group.shape[-1]
    n_experts = w_router_expert.shape[-1]
    assert aw % (2 * hd) == 0 and pw % len(POOL_WINDOWS) == 0
    assert n_groups + n_experts <= LANES and (n_experts // n_groups) & (n_experts // n_groups - 1) == 0

    trunks = []
    for x in (x_prompt, x_sample):
        b, s, _ = x.shape
        trunks.append(dict(batch=b, seq=s, x=x.reshape(b * s, D), bands=_pool_bands(min(s, 512))))
    cos_t, sx_t = _rope_tables(max(tr["seq"] for tr in trunks), hd, rot_dim)
    perm = _rope_perm(hd, rot_dim)
    t_a = trunks[0]["x"].shape[0]
    t_b = trunks[1]["x"].shape[0]
    tm_moe = min(512, t_a, t_b)
    tt_cmb = min(256, t_a, t_b)

    for l in range(n_layers):
        lam_init = _lambda_init(l)
        scale = hd ** -0.5 * math.log2(math.e)
        w_qk = w_in[l, :, :2 * aw].reshape(D, 2 * aw // hd, hd)[:, :, perm].reshape(D, 2 * aw)
        w_in_bf = jnp.concatenate([w_qk, w_in[l, :, 2 * aw:]], axis=1).astype(BF16)
        head_gain = jnp.concatenate([jnp.tile(q_norm[l][perm] * scale, aw // hd),
                                     jnp.tile(k_norm[l][perm], aw // hd)]).reshape(1, 2 * aw)
        lamv = jnp.stack([lambda_q1[l], lambda_k1[l], lambda_q2[l], lambda_k2[l]]).astype(F32)
        w_pool_bf = w_pool[l].astype(BF16)
        wa_bf = w_out[l, :aw].astype(BF16)
        wp_bf = w_out[l, aw:].astype(BF16)
        w_r = jnp.concatenate([w_router_group[l], w_router_expert[l]], axis=1)
        w_r = jnp.pad(w_r, ((0, 0), (0, LANES - w_r.shape[1])))
        w_r_hi = w_r.astype(BF16)
        w_r_lo = (w_r - w_r_hi.astype(F32)).astype(BF16)
        b_r = jnp.pad(jnp.concatenate([b_router_group[l], b_router_expert[l]]).astype(F32),
                      (0, LANES - n_groups - n_experts)).reshape(1, LANES)
        dff = w_gate.shape[-1]
        inproj_side = [w_gate[l].reshape(n_experts * D, dff), w_up[l].reshape(n_experts * D, dff)]
        outproj_side = [w_down[l].reshape(n_experts * dff, D), None]
        expert_w = []

        x1s = []
        for tr, side_in, side_out in zip(trunks, inproj_side, outproj_side):
            proj, side_bf = _inproj(tr["x"], norm_mix[l].reshape(1, D), w_in_bf, head_gain, cos_t, sx_t,
                                    side_in, seq=tr["seq"], aw=aw, hd=hd)
            expert_w.append(side_bf.reshape(n_experts, D, dff))
            a = _attention(proj, lamv, subln[l].reshape(1, 2 * hd), batch=tr["batch"], seq=tr["seq"],
                           aw=aw, hd=hd, lam_init=lam_init)
            p = _pool(proj, w_pool_bf, pool_scale[l].reshape(1, pw), tr["bands"], batch=tr["batch"],
                      seq=tr["seq"], aw=aw, pw=pw)
            if side_out is None:
                x1s.append(_outproj(a, p, wa_bf, wp_bf, tr["x"]))
            else:
                x1, side_bf = _outproj(a, p, wa_bf, wp_bf, tr["x"], side_out)
                x1s.append(x1)
                expert_w.append(side_bf.reshape(n_experts, dff, D))
        wg_bf, wd_bf, wu_bf = expert_w

        xn, eid, gate = _router(x1s[0], x1s[1], norm_ffn[l].reshape(1, D), w_r_hi, w_r_lo, b_r,
                                n_groups=n_groups, n_experts=n_experts)
        te, tlen, n_tiles, tok_tbl, pos, _ = _moe_plan(eid[:, :TOP_K], tm=tm_moe, n_experts=n_experts)
        ys = _experts(te, n_tiles, tlen, tok_tbl, xn, wg_bf, wu_bf, wd_bf, tm=tm_moe)
        t_all = t_a + t_b
        pos_tbl = pos.reshape(t_all // tt_cmb, tt_cmb, TOP_K).transpose(0, 2, 1).reshape(
            t_all // tt_cmb, TOP_K * tt_cmb)
        outs = []
        tile_off = 0
        for tr, x1 in zip(trunks, x1s):
            outs.append(_combine(pos_tbl, ys, x1, gate, tile_off=tile_off, tt=tt_cmb))
            tile_off += x1.shape[0] // tt_cmb
        for tr, o in zip(trunks, outs):
            tr["x"] = o

    return tuple(tr["x"].reshape(tr["batch"], tr["seq"], D) for tr in trunks)
```

```python
import functools
import math

import jax
import jax.numpy as jnp
import numpy as np
from jax import lax
from jax.experimental import pallas as pl
from jax.experimental.pallas import tpu as pltpu

EPS = 1e-6
ROPE_THETA = 500000.0
POOL_WINDOWS = (2, 4, 8, 16)
TOP_K = 2
POOL_HALO = 16
LANES = 128
VMEM_CAP_BYTES = 60000 * 1024
F32 = jnp.float32
BF16 = jnp.bfloat16


def _lambda_init(layer_idx):
    return 0.8 - 0.6 * math.exp(-0.3 * layer_idx)


def _vmem_limit(est_bytes):
    return int(min(VMEM_CAP_BYTES, est_bytes * 5 // 4 + (4 << 20)))


def _params(sem, est_bytes):
    return pltpu.CompilerParams(dimension_semantics=sem, vmem_limit_bytes=_vmem_limit(est_bytes))


def _cast_block_rows(rows, steps):
    rb = -(-rows // steps)
    while rows % rb or rb % 16:
        rb += 1
    return rb


def _expert_tiles(D, dff):
    return min(512, dff), min(2048, D)


def _cast_store(wc_ref, wc_out_ref):
    if len(wc_out_ref.shape) == 3:
        cw = wc_out_ref.shape[2]
        for c in range(wc_out_ref.shape[0]):
            wc_out_ref[c] = wc_ref[:, c * cw:(c + 1) * cw].astype(wc_out_ref.dtype)
    else:
        wc_out_ref[...] = wc_ref[...].astype(wc_out_ref.dtype)


def _cast_out_spec(rows, cols, rb, n_exp, chunk, flat_ix):
    rpe = rows // n_exp
    if rpe % rb == 0 and cols % chunk == 0:
        bpe = rpe // rb
        nck = cols // chunk

        def ix(i, j):
            t = flat_ix(i, j)
            return (t // bpe, 0, t % bpe, 0)

        return (pl.BlockSpec((None, nck, rb, chunk), ix),
                jax.ShapeDtypeStruct((n_exp, nck, rpe, chunk), BF16))
    return (pl.BlockSpec((rb, cols), lambda i, j: (flat_ix(i, j), 0)),
            jax.ShapeDtypeStruct((rows, cols), BF16))


def _inproj_kernel(x_ref, g_ref, w_ref, hg_ref, cos_ref, sx_ref, wc_ref, o_ref, wc_out_ref,
                   xn_ref, acc0_ref, acc1_ref, *, n_qk_tiles, n_col_tiles, hd):
    j = pl.program_id(1)
    tn = w_ref.shape[1]
    accs = (acc0_ref, acc1_ref)

    def cast_side():
        _cast_store(wc_ref, wc_out_ref)

    @pl.when(j == 0)
    def _():
        x = x_ref[...]
        ms = jnp.mean(x * x, axis=-1, keepdims=True)
        xn_ref[...] = (x * lax.rsqrt(ms + EPS) * g_ref[...]).astype(xn_ref.dtype)

    def matmul_into(acc_ref):
        acc_ref[...] = jnp.dot(xn_ref[...], w_ref[...], preferred_element_type=F32)

    def qk_epilogue(acc_ref):
        cos = cos_ref[...]
        sx = sx_ref[...]
        for c in range(tn // hd):
            blk = acc_ref[:, c * hd:(c + 1) * hd]
            ms = jnp.mean(blk * blk, axis=-1, keepdims=True)
            y = blk * lax.rsqrt(ms + EPS) * hg_ref[:, c * hd:(c + 1) * hd]
            y = y * cos + pltpu.roll(y, hd // 2, 1) * sx
            o_ref[:, c * hd:(c + 1) * hd] = y.astype(o_ref.dtype)

    def plain_epilogue(acc_ref):
        o_ref[...] = acc_ref[...].astype(o_ref.dtype)

    @pl.when(j == 0)
    def _():
        cast_side()
        matmul_into(accs[0])

    for parity in range(2):
        cur, prev = accs[parity], accs[1 - parity]
        is_par = (j % 2) == parity

        @pl.when(jnp.logical_and(is_par, jnp.logical_and(j >= 1, j <= n_qk_tiles)))
        def _(cur=cur, prev=prev):
            cast_side()
            matmul_into(cur)
            qk_epilogue(prev)

        @pl.when(jnp.logical_and(is_par, jnp.logical_and(j > n_qk_tiles, j < n_col_tiles)))
        def _(cur=cur, prev=prev):
            cast_side()
            matmul_into(cur)
            plain_epilogue(prev)

    @pl.when(j == n_col_tiles)
    def _():
        cast_side()
        plain_epilogue(accs[(n_col_tiles - 1) % 2])


def _inproj(x2, gain, w_bf, head_gain, cos_t, sx_t, side_w, *, seq, aw, hd, side_experts, side_chunk):
    T, D = x2.shape
    N = w_bf.shape[1]
    tm = min(512, seq)
    tn = min(1024, 2 * aw)
    n_qk_tiles = (2 * aw) // tn
    n_col_tiles = N // tn
    assert n_qk_tiles < n_col_tiles
    s_tiles = seq // tm
    n_steps = n_col_tiles + 1
    rows, cols = side_w.shape
    rb = _cast_block_rows(rows, (T // tm) * n_steps)
    cast_blocks = rows // rb
    flat_ix = lambda i, j: jnp.minimum(i * n_steps + j, cast_blocks - 1)
    cast_ix = lambda i, j: (flat_ix(i, j), 0)
    cast_out_spec, cast_out_shape = _cast_out_spec(rows, cols, rb, side_experts, side_chunk, flat_ix)
    est = (2 * tm * D * 4 + 2 * D * tn * 2 + tm * D * 2 + 2 * tm * tn * 2 + 4 * tm * tn * 4
           + 2 * rb * cols * (4 + 2))
    kern = functools.partial(_inproj_kernel, n_qk_tiles=n_qk_tiles, n_col_tiles=n_col_tiles, hd=hd)
    return pl.pallas_call(
        kern,
        grid=(T // tm, n_steps),
        in_specs=[
            pl.BlockSpec((tm, D), lambda i, j: (i, 0)),
            pl.BlockSpec((1, D), lambda i, j: (0, 0)),
            pl.BlockSpec((D, tn), lambda i, j: (0, jnp.minimum(j, n_col_tiles - 1))),
            pl.BlockSpec((1, tn), lambda i, j: (0, jnp.clip(j - 1, 0, n_qk_tiles - 1))),
            pl.BlockSpec((tm, hd), lambda i, j: (i % s_tiles, 0)),
            pl.BlockSpec((tm, hd), lambda i, j: (i % s_tiles, 0)),
            pl.BlockSpec((rb, cols), cast_ix),
        ],
        out_specs=[
            pl.BlockSpec((tm, tn), lambda i, j: (i, jnp.maximum(j - 1, 0))),
            cast_out_spec,
        ],
        out_shape=[jax.ShapeDtypeStruct((T, N), BF16), cast_out_shape],
        scratch_shapes=[pltpu.VMEM((tm, D), BF16), pltpu.VMEM((tm, tn), F32), pltpu.VMEM((tm, tn), F32)],
        compiler_params=_params(("arbitrary", "arbitrary"), est),
        name="inproj",
    )(x2, gain, w_bf, head_gain, cos_t, sx_t, side_w)


def _attn_kernel(lam_ref, subln_ref, q_ref, k_ref, v_ref, o_ref,
                 vt_ref, e_ref, mc_ref, lc_ref, acc_ref, *, tk, sub, hd, lam_init, unroll):
    seq = k_ref.shape[0]
    nck = seq // tk
    contract_last = (((1,), (1,)), ((), ()))

    @pl.when(pl.program_id(2) == 0)
    def _():
        def transpose_v(c, carry):
            vc = v_ref[pl.ds(pl.multiple_of(c * tk, tk), tk), :]
            vt_ref[c] = vc.astype(F32).T.astype(vt_ref.dtype)
            return carry

        lax.fori_loop(0, nck, transpose_v, 0)

    q = q_ref[...]
    tq = q.shape[0]
    nsub = tk // sub
    sl = 8

    def pass1(c, carry):
        kc = k_ref[pl.ds(pl.multiple_of(c * tk, tk), tk), :]
        for comp in range(2):
            st = lax.dot_general(kc[:, comp * hd:(comp + 1) * hd], q[:, comp * hd:(comp + 1) * hd],
                                 contract_last, preferred_element_type=F32)
            for j in range(nsub):
                sj = st[j * sub:(j + 1) * sub].reshape(sub // sl, sl, tq)
                mj = jnp.max(sj, axis=0)
                ej = jnp.exp2(sj - mj[None])
                mc_ref[comp, c, j] = mj
                lc_ref[comp, c, j] = jnp.sum(ej, axis=0)
                e_ref[comp, c, j * sub:(j + 1) * sub, :] = ej.reshape(sub, tq).astype(e_ref.dtype)
        return carry

    lax.fori_loop(0, nck, pass1, 0, unroll=unroll)

    lv = lam_ref[...]
    lam = (jnp.exp(jnp.sum(lv[0:1] * lv[1:2], axis=1, keepdims=True))
           - jnp.exp(jnp.sum(lv[2:3] * lv[3:4], axis=1, keepdims=True)) + lam_init)
    def over_blocks(red, x):
        return red(red(red(x, axis=0), axis=0), axis=0, keepdims=True)

    m0 = over_blocks(jnp.max, mc_ref[0])
    m1 = over_blocks(jnp.max, mc_ref[1])
    r0 = 1.0 / over_blocks(jnp.sum, lc_ref[0] * jnp.exp2(mc_ref[0] - m0))
    r1 = lam / over_blocks(jnp.sum, lc_ref[1] * jnp.exp2(mc_ref[1] - m1))
    acc_ref[...] = jnp.zeros(acc_ref.shape, F32)

    def pass2(c, carry):
        f0 = jnp.exp2(mc_ref[0, c] - m0) * r0
        f1 = jnp.exp2(mc_ref[1, c] - m1) * r1
        pk = 2 * sl
        blocks = []
        for j in range(nsub):
            g0 = jnp.concatenate([f0[j], f0[j]], axis=0).astype(e_ref.dtype)
            g1 = jnp.concatenate([f1[j], f1[j]], axis=0).astype(e_ref.dtype)
            e0 = e_ref[0, c, j * sub:(j + 1) * sub, :].reshape(sub // pk, pk, tq)
            e1 = e_ref[1, c, j * sub:(j + 1) * sub, :].reshape(sub // pk, pk, tq)
            blocks.append((e0 * g0[None] - e1 * g1[None]).reshape(sub, tq))
        wt = jnp.concatenate(blocks, axis=0)
        acc_ref[...] += jnp.dot(vt_ref[c], wt, preferred_element_type=F32)
        return carry

    lax.fori_loop(0, nck, pass2, 0, unroll=unroll)

    o = acc_ref[...].T
    ms = jnp.mean(o * o, axis=-1, keepdims=True)
    o = o * lax.rsqrt(ms + EPS) * subln_ref[...]
    o_ref[...] = (o * (1.0 - lam_init)).astype(o_ref.dtype)


def _attention(proj, lamv, subln, *, batch, seq, aw, hd, lam_init):
    T = proj.shape[0]
    vd = 2 * hd
    n_heads = aw // vd
    tq = min(seq, 512)
    tk = min(seq, 512)
    nck = seq // tk
    q_tiles = seq // tq
    k_col0 = aw // vd
    v_col0 = 2 * aw // vd
    sub = min(tk, 128)
    nsub = tk // sub
    est = (2 * 2 * seq * vd * 2 + seq * vd * 2 + 2 * seq * tq * 2 + 2 * 2 * nck * nsub * 8 * tq * 4
           + tq * vd * 4 + 4 * tq * vd * 2 + 6 * tq * tk * 4)
    kern = functools.partial(_attn_kernel, tk=tk, sub=sub, hd=hd, lam_init=lam_init, unroll=min(8, nck))
    return pl.pallas_call(
        kern,
        grid=(batch, n_heads, q_tiles),
        in_specs=[
            pl.BlockSpec((4, hd), lambda b, h, i: (0, 0)),
            pl.BlockSpec((1, vd), lambda b, h, i: (0, 0)),
            pl.BlockSpec((tq, vd), lambda b, h, i: (b * q_tiles + i, h)),
            pl.BlockSpec((seq, vd), lambda b, h, i: (b, k_col0 + h)),
            pl.BlockSpec((seq, vd), lambda b, h, i: (b, v_col0 + h)),
        ],
        out_specs=pl.BlockSpec((tq, vd), lambda b, h, i: (b * q_tiles + i, h)),
        out_shape=jax.ShapeDtypeStruct((T, aw), BF16),
        scratch_shapes=[
            pltpu.VMEM((nck, vd, tk), BF16),
            pltpu.VMEM((2, nck, tk, tq), BF16),
            pltpu.VMEM((2, nck, nsub, 8, tq), F32),
            pltpu.VMEM((2, nck, nsub, 8, tq), F32),
            pltpu.VMEM((vd, tq), F32),
        ],
        compiler_params=_params(("arbitrary", "arbitrary", "arbitrary"), est),
        name="diff_attention",
    )(lamv, subln, proj, proj, proj)


def _pool_bands(rows):
    t = np.arange(rows)[:, None]
    jm = np.arange(rows)[None, :]
    jh = np.arange(POOL_HALO)[None, :]
    bm, bp, bn = [], [], []
    for w in POOL_WINDOWS:
        left = w // 2
        right = w - 1 - left
        bm.append((jm >= t - left) & (jm <= t + right))
        bp.append(jh - POOL_HALO >= t - left)
        bn.append(rows + jh <= t + right)
    as_bf = lambda m: jnp.asarray(np.stack(m).astype(np.float32), dtype=BF16)
    return as_bf(bm), as_bf(bp), as_bf(bn)


def _pool_kernel(u_ref, wp_ref, ps_ref, bm_ref, bp_ref, bn_ref, o_ref, *, rows):
    g = pl.program_id(1)
    seq = u_ref.shape[0]
    n_chunks = seq // rows
    for gi, w in enumerate(POOL_WINDOWS):
        left = w // 2
        right = w - 1 - left

        @pl.when(g == gi)
        def _(gi=gi, left=left, right=right):
            def body(r, carry):
                r0 = pl.multiple_of(r * rows, rows)
                main = u_ref[pl.ds(r0, rows), :]
                p0 = pl.multiple_of(jnp.maximum(r0 - POOL_HALO, 0), POOL_HALO)
                n0 = pl.multiple_of(jnp.minimum(r0 + rows, seq - POOL_HALO), POOL_HALO)
                prev = u_ref[pl.ds(p0, POOL_HALO), :]
                nxt = u_ref[pl.ds(n0, POOL_HALO), :]
                prev = jnp.where(r > 0, prev, jnp.zeros_like(prev))
                nxt = jnp.where(r < n_chunks - 1, nxt, jnp.zeros_like(nxt))
                ssum = (jnp.dot(bm_ref[gi], main, preferred_element_type=F32)
                        + jnp.dot(bp_ref[gi], prev, preferred_element_type=F32)
                        + jnp.dot(bn_ref[gi], nxt, preferred_element_type=F32))
                t = r0 + lax.broadcasted_iota(jnp.int32, (rows, 1), 0)
                lo = jnp.maximum(t - left, 0)
                hi = jnp.minimum(t + right, seq - 1)
                cnt = (hi - lo + 1).astype(F32)
                delta = ssum / cnt - main.astype(F32)
                y = jnp.dot(delta.astype(wp_ref.dtype), wp_ref[...], preferred_element_type=F32)
                o_ref[pl.ds(r0, rows), :] = (y * ps_ref[...]).astype(o_ref.dtype)
                return carry

            lax.fori_loop(0, n_chunks, body, 0)


def _pool(proj, w_pool_bf, pool_scale, bands, *, batch, seq, aw, pw):
    T = proj.shape[0]
    n_groups, gw, _ = w_pool_bf.shape
    rows = min(seq, 512)
    u_col0 = 3 * aw // gw
    bm, bp, bn = bands
    est = (2 * 2 * seq * gw * 2 + 2 * gw * gw * 2 + 2 * 4 * rows * (rows + 2 * POOL_HALO) * 2
           + 8 * rows * gw * 4)
    kern = functools.partial(_pool_kernel, rows=rows)
    return pl.pallas_call(
        kern,
        grid=(batch, n_groups),
        in_specs=[
            pl.BlockSpec((seq, gw), lambda b, g: (b, u_col0 + g)),
            pl.BlockSpec((None, gw, gw), lambda b, g: (g, 0, 0)),
            pl.BlockSpec((1, gw), lambda b, g: (0, g)),
            pl.BlockSpec(bm.shape, lambda b, g: (0, 0, 0)),
            pl.BlockSpec(bp.shape, lambda b, g: (0, 0, 0)),
            pl.BlockSpec(bn.shape, lambda b, g: (0, 0, 0)),
        ],
        out_specs=pl.BlockSpec((seq, gw), lambda b, g: (b, g)),
        out_shape=jax.ShapeDtypeStruct((T, pw), BF16),
        compiler_params=_params(("arbitrary", "arbitrary"), est),
        name="multiscale_pool",
    )(proj, w_pool_bf, pool_scale, bm, bp, bn)


def _outproj_kernel(a_ref, p_ref, wa_ref, wp_ref, x_ref, *rest):
    if len(rest) == 3:
        wc_ref, o_ref, wc_out_ref = rest
        _cast_store(wc_ref, wc_out_ref)
    else:
        (o_ref,) = rest
    acc = (jnp.dot(a_ref[...], wa_ref[...], preferred_element_type=F32)
           + jnp.dot(p_ref[...], wp_ref[...], preferred_element_type=F32))
    o_ref[...] = x_ref[...] + acc


def _outproj(a, p, wa_bf, wp_bf, x2, side_w=None, side_experts=1, side_chunk=1):
    T, D = x2.shape
    aw = a.shape[1]
    pw = p.shape[1]
    tm = min(512, T)
    tn = min(1024, D)
    n_col = D // tn
    est = 2 * (tm * aw * 2 + tm * pw * 2 + (aw + pw) * tn * 2 + 2 * tm * tn * 4) + 2 * tm * tn * 4
    in_specs = [
        pl.BlockSpec((tm, aw), lambda i, j: (i, 0)),
        pl.BlockSpec((tm, pw), lambda i, j: (i, 0)),
        pl.BlockSpec((aw, tn), lambda i, j: (0, j)),
        pl.BlockSpec((pw, tn), lambda i, j: (0, j)),
        pl.BlockSpec((tm, tn), lambda i, j: (i, j)),
    ]
    out_specs = [pl.BlockSpec((tm, tn), lambda i, j: (i, j))]
    out_shape = [jax.ShapeDtypeStruct((T, D), F32)]
    args = [a, p, wa_bf, wp_bf, x2]
    if side_w is not None:
        rows, cols = side_w.shape
        rb = _cast_block_rows(rows, (T // tm) * n_col)
        cast_blocks = rows // rb
        flat_ix = lambda i, j: jnp.minimum(i * n_col + j, cast_blocks - 1)
        cast_out_spec, cast_out_shape = _cast_out_spec(rows, cols, rb, side_experts, side_chunk, flat_ix)
        in_specs.append(pl.BlockSpec((rb, cols), lambda i, j: (flat_ix(i, j), 0)))
        out_specs.append(cast_out_spec)
        out_shape.append(cast_out_shape)
        args.append(side_w)
        est += 2 * rb * cols * (4 + 2)
    res = pl.pallas_call(
        _outproj_kernel,
        grid=(T // tm, n_col),
        in_specs=in_specs,
        out_specs=out_specs,
        out_shape=out_shape,
        compiler_params=_params(("arbitrary", "arbitrary"), est),
        name="outproj",
    )(*args)
    return res if side_w is not None else res[0]


def _router_kernel(xa_ref, xb_ref, g_ref, whi_ref, wlo_ref, b_ref, xn_ref, eid_ref, gate_ref, *,
                   a_tiles, n_groups, n_experts):
    i = pl.program_id(0)
    epg_shift = int(math.log2(n_experts // n_groups))

    def run(x_ref):
        x = x_ref[...]
        ms = jnp.mean(x * x, axis=-1, keepdims=True)
        xn = x * lax.rsqrt(ms + EPS) * g_ref[...]
        hi = xn.astype(BF16)
        xn_ref[...] = hi.astype(F32)
        lo = (xn - hi.astype(F32)).astype(BF16)
        whi = whi_ref[...]
        logits = (jnp.dot(hi, whi, preferred_element_type=F32)
                  + jnp.dot(lo, whi, preferred_element_type=F32)
                  + jnp.dot(hi, wlo_ref[...], preferred_element_type=F32)
                  + b_ref[...])
        lane = lax.broadcasted_iota(jnp.int32, logits.shape, 1)
        neg = jnp.float32(-jnp.inf)
        is_g = lane < n_groups
        gl = jnp.where(is_g, logits, neg)
        gmax = jnp.max(gl, axis=1, keepdims=True)
        g_idx = jnp.min(jnp.where(gl == gmax, lane, LANES), axis=1, keepdims=True)
        gsum = jnp.sum(jnp.where(is_g, jnp.exp(gl - gmax), 0.0), axis=1, keepdims=True)
        g_w = 1.0 / gsum
        e_lane = lane - n_groups
        grp = jnp.where(jnp.logical_and(e_lane >= 0, e_lane < n_experts),
                        lax.shift_right_arithmetic(e_lane, epg_shift), -1)
        sel = grp == g_idx
        el = jnp.where(sel, logits, neg)
        emax = jnp.max(el, axis=1, keepdims=True)
        ex = jnp.where(sel, jnp.exp(el - emax), 0.0)
        prob = ex / jnp.sum(ex, axis=1, keepdims=True)
        pm = jnp.where(sel, prob, -1.0)
        p1 = jnp.max(pm, axis=1, keepdims=True)
        i1 = jnp.min(jnp.where(pm == p1, lane, LANES), axis=1, keepdims=True)
        pm2 = jnp.where(lane == i1, -1.0, pm)
        p2 = jnp.max(pm2, axis=1, keepdims=True)
        i2 = jnp.min(jnp.where(pm2 == p2, lane, LANES), axis=1, keepdims=True)
        tsum = p1 + p2
        gate1 = g_w * (p1 / tsum)
        gate2 = g_w * (p2 / tsum)
        eid_ref[...] = jnp.where(lane == 0, i1 - n_groups, jnp.where(lane == 1, i2 - n_groups, 0))
        gate_ref[...] = jnp.where(lane == 0, gate1, jnp.where(lane == 1, gate2, 0.0))

    @pl.when(i < a_tiles)
    def _():
        run(xa_ref)

    @pl.when(i >= a_tiles)
    def _():
        run(xb_ref)


def _router(x1a, x1b, gain, whi, wlo, bias, *, n_groups, n_experts):
    Ta, D = x1a.shape
    Tb = x1b.shape[0]
    tt = min(256, Ta, Tb)
    a_tiles = Ta // tt
    b_tiles = Tb // tt
    T = Ta + Tb
    est = 2 * 2 * tt * D * 4 + 2 * tt * D * 4 + 4 * D * LANES * 2 + 6 * tt * D * 4
    kern = functools.partial(_router_kernel, a_tiles=a_tiles, n_groups=n_groups, n_experts=n_experts)
    return pl.pallas_call(
        kern,
        grid=(a_tiles + b_tiles,),
        in_specs=[
            pl.BlockSpec((tt, D), lambda i: (jnp.minimum(i, a_tiles - 1), 0)),
            pl.BlockSpec((tt, D), lambda i: (jnp.maximum(i - a_tiles, 0), 0)),
            pl.BlockSpec((1, D), lambda i: (0, 0)),
            pl.BlockSpec((D, LANES), lambda i: (0, 0)),
            pl.BlockSpec((D, LANES), lambda i: (0, 0)),
            pl.BlockSpec((1, LANES), lambda i: (0, 0)),
        ],
        out_specs=[
            pl.BlockSpec((tt, D), lambda i: (i, 0)),
            pl.BlockSpec((tt, LANES), lambda i: (i, 0)),
            pl.BlockSpec((tt, LANES), lambda i: (i, 0)),
        ],
        out_shape=[
            jax.ShapeDtypeStruct((T, D), F32),
            jax.ShapeDtypeStruct((T, LANES), jnp.int32),
            jax.ShapeDtypeStruct((T, LANES), F32),
        ],
        compiler_params=_params(("arbitrary",), est),
        name="router",
    )(x1a, x1b, gain, whi, wlo, bias)


def _moe_plan(eid2, *, tm, n_experts):
    T = eid2.shape[0]
    A = T * TOP_K
    eflat = eid2.reshape(A)
    experts = jnp.arange(n_experts, dtype=jnp.int32)
    sorted_e, order = lax.sort_key_val(eflat, jnp.arange(A, dtype=jnp.int32))
    counts = jnp.sum((eflat[None, :] == experts[:, None]).astype(jnp.int32), axis=1)
    ends = jnp.cumsum(counts).astype(jnp.int32)
    starts = ends - counts
    ntile_e = (counts + tm - 1) // tm
    tile_end = jnp.cumsum(ntile_e).astype(jnp.int32)
    tile_base = tile_end - ntile_e
    n_tiles = tile_end[-1]
    G = -(-A // tm) + n_experts
    ti = jnp.arange(G, dtype=jnp.int32)
    valid = ti < n_tiles
    te = jnp.sum((ti[:, None] >= tile_end[None, :]).astype(jnp.int32), axis=1)
    te = jnp.minimum(te, n_experts - 1)
    te = jnp.where(valid, te, te[n_tiles - 1])
    j = ti - tile_base[te]
    tstart = starts[te] + j * tm
    tlen = jnp.where(valid, jnp.clip(counts[te] - j * tm, 0, tm), 0).astype(jnp.int32)
    r = jnp.arange(tm, dtype=jnp.int32)
    sidx = jnp.clip(tstart[:, None] + r[None, :], 0, A - 1)
    tok_tbl = jnp.where(r[None, :] < tlen[:, None], order[sidx] // TOP_K, 0).astype(jnp.int32)
    rank = jnp.arange(A, dtype=jnp.int32) - starts[sorted_e]
    row = tile_base[sorted_e] * tm + rank
    _, pos = lax.sort_key_val(order, row)
    pos = pos.reshape(T, TOP_K)
    return te, tlen, n_tiles.reshape(1), tok_tbl, pos, G


def _expert_kernel(te_ref, nt_ref, tl_ref, tbl_ref, xn_ref, wg_ref, wu_ref, wd_ref, o_ref,
                   idx_ref, xrow_ref, h_ref, sem_idx, sem_rows, *,
                   n_tiles, fa, tf, half, grp):
    del te_ref, nt_ref
    i = pl.program_id(0)
    s = pl.program_id(1)
    tm = xrow_ref.shape[0]
    n_grp = tm // grp
    ln = tl_ref[i]

    def idx_copy(tile):
        dst = idx_ref.at[pl.ds(pl.multiple_of((tile % 2) * tm, tm), tm)]
        return pltpu.make_async_copy(tbl_ref.at[tile], dst, sem_idx.at[tile % 2])

    def issue_rows(tile):
        cnt = tl_ref[tile]
        base = (tile % 2) * tm
        for g in range(n_grp):
            @pl.when(g * grp < cnt)
            def _(g=g):
                for r in range(g * grp, (g + 1) * grp):
                    tok = idx_ref[base + r]
                    pltpu.make_async_copy(xn_ref.at[tok], xrow_ref.at[r], sem_rows).start()

    def wait_rows():
        for g in range(n_grp):
            @pl.when(g * grp < ln)
            def _(g=g):
                blk = xrow_ref.at[pl.ds(g * grp, grp)]
                pltpu.make_async_copy(blk, blk, sem_rows).wait()

    @pl.when(jnp.logical_and(i == 0, s == 0))
    def _():
        xrow_ref[...] = jnp.zeros(xrow_ref.shape, xrow_ref.dtype)
        first = idx_copy(0)
        first.start()
        first.wait()
        issue_rows(0)
        if n_tiles > 1:
            idx_copy(1).start()

    @pl.when(s == 0)
    def _():
        wait_rows()

    @pl.when(jnp.logical_and(s == fa, i + 1 < n_tiles))
    def _():
        idx_copy(i + 1).wait()
        issue_rows(i + 1)

        @pl.when(i + 2 < n_tiles)
        def _():
            idx_copy(i + 2).start()

    def rows_variants(fn):
        @pl.when(ln > half)
        def _():
            fn(tm)

        @pl.when(jnp.logical_and(ln > 0, ln <= half))
        def _():
            fn(half)

    for f in range(fa):
        @pl.when(s == f)
        def _(f=f):
            def gate_up(nrows):
                x = xrow_ref[0:nrows, :].astype(wg_ref.dtype)
                gate = jnp.dot(x, wg_ref[...], preferred_element_type=F32)
                up = jnp.dot(x, wu_ref[...], preferred_element_type=F32)
                h_ref[0:nrows, f * tf:(f + 1) * tf] = (jax.nn.silu(gate) * up).astype(h_ref.dtype)

            rows_variants(gate_up)

    @pl.when(s >= fa)
    def _():
        def down(nrows):
            o_ref[0:nrows, :] = jnp.dot(h_ref[0:nrows, :], wd_ref[...], preferred_element_type=F32)

        rows_variants(down)

        @pl.when(ln <= half)
        def _():
            o_ref[half:tm, :] = jnp.zeros((tm - half, o_ref.shape[1]), o_ref.dtype)

        @pl.when(ln == 0)
        def _():
            o_ref[0:half, :] = jnp.zeros((half, o_ref.shape[1]), o_ref.dtype)


def _experts(te, n_tiles, tlen, tok_tbl, xn, wg_bf, wu_bf, wd_bf, *, tm):
    G = te.shape[0]
    D = xn.shape[1]
    dff = wg_bf.size // (wg_bf.shape[0] * D)
    tf, tn = _expert_tiles(D, dff)
    fa = dff // tf
    fb = D // tn
    half = tm // 2
    grp = min(64, tm)
    est = (tm * D * 4 + tm * D * 2 + tm * dff * 2 + 2 * 2 * D * tf * 2 + 2 * dff * tn * 2
           + 2 * tm * tn * 4 + 3 * tm * tf * 4 + tm * tn * 4)

    def a_ix(i, s, nt):
        return jnp.where(i < nt[0], jnp.minimum(s, fa - 1), fa - 1)

    def b_ix(s):
        return jnp.clip(s - fa, 0, fb - 1)

    def wd_ix(i, s, nt):
        return jnp.where(i < nt[0], b_ix(s), fb - 1)

    def w_spec(w, rows, width, ix):
        if w.ndim == 4:
            assert w.shape[2:] == (rows, width)
            return pl.BlockSpec((None, None, rows, width),
                                lambda i, s, te, nt, tl: (te[i], ix(i, s, nt), 0, 0))
        return pl.BlockSpec((None, rows, width), lambda i, s, te, nt, tl: (te[i], 0, ix(i, s, nt)))

    kern = functools.partial(_expert_kernel, n_tiles=G, fa=fa, tf=tf, half=half, grp=grp)
    return pl.pallas_call(
        kern,
        grid_spec=pltpu.PrefetchScalarGridSpec(
            num_scalar_prefetch=3,
            grid=(G, fa + fb),
            in_specs=[
                pl.BlockSpec(memory_space=pl.ANY),
                pl.BlockSpec(memory_space=pl.ANY),
                w_spec(wg_bf, D, tf, a_ix),
                w_spec(wu_bf, D, tf, a_ix),
                w_spec(wd_bf, dff, tn, wd_ix),
            ],
            out_specs=pl.BlockSpec((tm, tn), lambda i, s, te, nt, tl: (i, b_ix(s))),
            scratch_shapes=[
                pltpu.SMEM((2 * tm,), jnp.int32),
                pltpu.VMEM((tm, D), F32),
                pltpu.VMEM((tm, dff), BF16),
                pltpu.SemaphoreType.DMA((2,)),
                pltpu.SemaphoreType.DMA(()),
            ],
        ),
        out_shape=jax.ShapeDtypeStruct((G * tm, D), F32),
        compiler_params=_params(("arbitrary", "arbitrary"), est),
        name="moe_experts",
    )(te, n_tiles, tlen, tok_tbl, xn, wg_bf, wu_bf, wd_bf)


def _combine_kernel(pos_ref, ys_ref, x1_ref, gate_ref, o_ref, idx_ref, ybuf_ref, sem_idx, sem_rows, *,
                    tile_off, n_tiles, tt):
    i = pl.program_id(0)
    slot = i % 2

    def idx_copy(tile, s):
        return pltpu.make_async_copy(pos_ref.at[tile + tile_off], idx_ref.at[s], sem_idx.at[s])

    def issue_rows(s):
        def body(r, carry):
            row = idx_ref[s, r]
            pltpu.make_async_copy(ys_ref.at[row], ybuf_ref.at[s, r], sem_rows.at[s]).start()
            return carry

        lax.fori_loop(0, TOP_K * tt, body, 0, unroll=8)

    @pl.when(i == 0)
    def _():
        first = idx_copy(0, 0)
        first.start()
        first.wait()
        issue_rows(0)
        if n_tiles > 1:
            idx_copy(1, 1).start()

    @pl.when(i + 1 < n_tiles)
    def _():
        idx_copy(i + 1, 1 - slot).wait()
        issue_rows(1 - slot)

        @pl.when(i + 2 < n_tiles)
        def _():
            idx_copy(i + 2, slot).start()

    buf = ybuf_ref.at[slot]
    pltpu.make_async_copy(buf, buf, sem_rows.at[slot]).wait()
    g = gate_ref[...]
    y0 = ybuf_ref[slot, 0:tt, :]
    y1 = ybuf_ref[slot, tt:2 * tt, :]
    o_ref[...] = x1_ref[...] + (y0 * g[:, 0:1] + y1 * g[:, 1:2])


def _combine(pos_tbl, ys, x1, gate, *, tile_off, tt):
    T, D = x1.shape
    n_tiles = T // tt
    est = 2 * TOP_K * tt * D * 4 + 2 * 2 * tt * D * 4 + 2 * tt * LANES * 4 + 3 * tt * D * 4
    kern = functools.partial(_combine_kernel, tile_off=tile_off, n_tiles=n_tiles, tt=tt)
    return pl.pallas_call(
        kern,
        grid=(n_tiles,),
        in_specs=[
            pl.BlockSpec(memory_space=pl.ANY),
            pl.BlockSpec(memory_space=pl.ANY),
            pl.BlockSpec((tt, D), lambda i: (i, 0)),
            pl.BlockSpec((tt, LANES), lambda i: (i + tile_off, 0)),
        ],
        out_specs=pl.BlockSpec((tt, D), lambda i: (i, 0)),
        out_shape=jax.ShapeDtypeStruct((T, D), F32),
        scratch_shapes=[
            pltpu.SMEM((2, TOP_K * tt), jnp.int32),
            pltpu.VMEM((2, TOP_K * tt, D), F32),
            pltpu.SemaphoreType.DMA((2,)),
            pltpu.SemaphoreType.DMA((2,)),
        ],
        compiler_params=_params(("arbitrary",), est),
        name="moe_combine",
    )(pos_tbl, ys, x1, gate)


def _rope_perm(hd, rot_dim):
    half = rot_dim // 2
    mid = hd // 2
    assert rot_dim <= mid
    perm = np.arange(hd)
    perm[half:rot_dim] = np.arange(mid, mid + half)
    perm[mid:mid + half] = np.arange(half, rot_dim)
    return perm


def _rope_tables(seq, hd, rot_dim):
    half = rot_dim // 2
    mid = hd // 2
    inv_freq = ROPE_THETA ** (-jnp.arange(half, dtype=F32) * 2.0 / rot_dim)
    ang = jnp.arange(seq, dtype=F32)[:, None] * inv_freq[None, :]
    cos, sin = jnp.cos(ang), jnp.sin(ang)
    ones = lambda n: jnp.ones((seq, n), F32)
    zeros = lambda n: jnp.zeros((seq, n), F32)
    cos_t = jnp.concatenate([cos, ones(mid - half), cos, ones(hd - mid - half)], axis=1)
    sx_t = jnp.concatenate([-sin, zeros(mid - half), sin, zeros(hd - mid - half)], axis=1)
    return cos_t, sx_t


def kernel(x_prompt, x_sample, norm_mix, w_in, q_norm, k_norm, lambda_q1, lambda_k1, lambda_q2,
           lambda_k2, subln, w_pool, pool_scale, w_out, norm_ffn, w_router_group, b_router_group,
           w_router_expert, b_router_expert, w_gate, w_up, w_down):
    n_layers, D, in_cols = w_in.shape
    mix = w_out.shape[1]
    aw = (in_cols - mix) // 2
    pw = mix - aw
    hd = q_norm.shape[-1]
    rot_dim = hd // 4
    n_groups = w_router_group.shape[-1]
    n_experts = w_router_expert.shape[-1]
    assert aw % (2 * hd) == 0 and pw % len(POOL_WINDOWS) == 0
    assert n_groups + n_experts <= LANES and (n_experts // n_groups) & (n_experts // n_groups - 1) == 0

    trunks = []
    for x in (x_prompt, x_sample):
        b, s, _ = x.shape
        trunks.append(dict(batch=b, seq=s, x=x.reshape(b * s, D), bands=_pool_bands(min(s, 512))))
    cos_t, sx_t = _rope_tables(max(tr["seq"] for tr in trunks), hd, rot_dim)
    perm = _rope_perm(hd, rot_dim)
    t_a = trunks[0]["x"].shape[0]
    t_b = trunks[1]["x"].shape[0]
    tm_moe = min(512, t_a, t_b)
    tt_cmb = min(256, t_a, t_b)

    for l in range(n_layers):
        lam_init = _lambda_init(l)
        scale = hd ** -0.5 * math.log2(math.e)
        w_qk = w_in[l, :, :2 * aw].reshape(D, 2 * aw // hd, hd)[:, :, perm].reshape(D, 2 * aw)
        w_in_bf = jnp.concatenate([w_qk, w_in[l, :, 2 * aw:]], axis=1).astype(BF16)
        head_gain = jnp.concatenate([jnp.tile(q_norm[l][perm] * scale, aw // hd),
                                     jnp.tile(k_norm[l][perm], aw // hd)]).reshape(1, 2 * aw)
        lamv = jnp.stack([lambda_q1[l], lambda_k1[l], lambda_q2[l], lambda_k2[l]]).astype(F32)
        w_pool_bf = w_pool[l].astype(BF16)
        wa_bf = w_out[l, :aw].astype(BF16)
        wp_bf = w_out[l, aw:].astype(BF16)
        w_r = jnp.concatenate([w_router_group[l], w_router_expert[l]], axis=1)
        w_r = jnp.pad(w_r, ((0, 0), (0, LANES - w_r.shape[1])))
        w_r_hi = w_r.astype(BF16)
        w_r_lo = (w_r - w_r_hi.astype(F32)).astype(BF16)
        b_r = jnp.pad(jnp.concatenate([b_router_group[l], b_router_expert[l]]).astype(F32),
                      (0, LANES - n_groups - n_experts)).reshape(1, LANES)
        dff = w_gate.shape[-1]
        tf_e, tn_e = _expert_tiles(D, dff)
        inproj_side = [w_gate[l].reshape(n_experts * D, dff), w_up[l].reshape(n_experts * D, dff)]
        outproj_side = [w_down[l].reshape(n_experts * dff, D), None]
        expert_w = []

        x1s = []
        for tr, side_in, side_out in zip(trunks, inproj_side, outproj_side):
            proj, side_bf = _inproj(tr["x"], norm_mix[l].reshape(1, D), w_in_bf, head_gain, cos_t, sx_t,
                                    side_in, seq=tr["seq"], aw=aw, hd=hd, side_experts=n_experts,
                                    side_chunk=tf_e)
            expert_w.append(side_bf if side_bf.ndim == 4 else side_bf.reshape(n_experts, D, dff))
            a = _attention(proj, lamv, subln[l].reshape(1, 2 * hd), batch=tr["batch"], seq=tr["seq"],
                           aw=aw, hd=hd, lam_init=lam_init)
            p = _pool(proj, w_pool_bf, pool_scale[l].reshape(1, pw), tr["bands"], batch=tr["batch"],
                      seq=tr["seq"], aw=aw, pw=pw)
            if side_out is None:
                x1s.append(_outproj(a, p, wa_bf, wp_bf, tr["x"]))
            else:
                x1, side_bf = _outproj(a, p, wa_bf, wp_bf, tr["x"], side_out, n_experts, tn_e)
                x1s.append(x1)
                expert_w.append(side_bf if side_bf.ndim == 4 else side_bf.reshape(n_experts, dff, D))
        wg_bf, wd_bf, wu_bf = expert_w

        xn, eid, gate = _router(x1s[0], x1s[1], norm_ffn[l].reshape(1, D), w_r_hi, w_r_lo, b_r,
                                n_groups=n_groups, n_experts=n_experts)
        te, tlen, n_tiles, tok_tbl, pos, _ = _moe_plan(eid[:, :TOP_K], tm=tm_moe, n_experts=n_experts)
        ys = _experts(te, n_tiles, tlen, tok_tbl, xn, wg_bf, wu_bf, wd_bf, tm=tm_moe)
        t_all = t_a + t_b
        pos_tbl = pos.reshape(t_all // tt_cmb, tt_cmb, TOP_K).transpose(0, 2, 1).reshape(
            t_all // tt_cmb, TOP_K * tt_cmb)
        outs = []
        tile_off = 0
        for tr, x1 in zip(trunks, x1s):
            outs.append(_combine(pos_tbl, ys, x1, gate, tile_off=tile_off, tt=tt_cmb))
            tile_off += x1.shape[0] // tt_cmb
        for tr, o in zip(trunks, outs):
            tr["x"] = o

    return tuple(tr["x"].reshape(tr["batch"], tr["seq"], D) for tr in trunks)
```
